```python
import math
import jax, jax.numpy as jnp
from jax import lax
import numpy as np

D_MODEL = 1024
BATCH = 4
SEQ = 8192
DEPTH = 1
DEC_BATCH = 2
DEC_SEQ = 8192
PAST_LEN = 128

N_HEADS = 16
N_KV_HEADS = 4
HEAD_DIM = 64
GQA_GROUP = N_HEADS // N_KV_HEADS
ATTN_DIM = N_HEADS * HEAD_DIM
KV_DIM = N_KV_HEADS * HEAD_DIM
AXIS_ROT_DIM = HEAD_DIM // 2
ROPE_THETA = 10000.0
GRID_W = 64
Q_BLOCK = 128

SSM_EXPAND = 2
D_INNER = SSM_EXPAND * D_MODEL
SSM_HEADDIM = 64
N_SSM_HEADS = D_INNER // SSM_HEADDIM
N_SSM_GROUPS = 4
HEADS_PER_GROUP = N_SSM_HEADS // N_SSM_GROUPS
D_STATE = 128
SSM_CONV = 3
CONV_DIM = D_INNER + 2 * N_SSM_GROUPS * D_STATE
CHUNK = 128

D_FF = 2816
FFN_CONV = 3

PLE_DIM = 256

NORM_EPS = 1e-6

IN_SIZES = (ATTN_DIM, KV_DIM, KV_DIM, D_INNER, CONV_DIM, 2 * N_SSM_HEADS, 2 * D_MODEL)
IN_DIM = ATTN_DIM + 2 * KV_DIM + D_INNER + CONV_DIM + 2 * N_SSM_HEADS + 2 * D_MODEL

kernel_name = 'hybrid_ssd_axial_gqa_encoder'


def _rms_norm(x, w):
    xf = x.astype(jnp.float32)
    y = xf * lax.rsqrt(jnp.mean(xf * xf, axis=-1, keepdims=True) + NORM_EPS) * w.astype(jnp.float32)
    return y.astype(x.dtype)


def _depthwise_conv(x, w, b):
    k = w.shape[0]
    y = lax.conv_general_dilated(
        x, w[:, None, :], window_strides=(1,), padding=[(k // 2, k // 2)],
        dimension_numbers=('NWC', 'WIO', 'NWC'), feature_group_count=x.shape[-1])
    return y + b


def _split_columns(proj):
    parts = []
    start = 0
    for size in IN_SIZES:
        parts.append(proj[..., start:start + size])
        start += size
    return parts


def _axial_rope(n_tokens):
    rows = n_tokens // GRID_W
    row_idx = jnp.repeat(jnp.arange(rows, dtype=jnp.float32), GRID_W)
    col_idx = jnp.tile(jnp.arange(GRID_W, dtype=jnp.float32), rows)
    inv_freq = ROPE_THETA ** (-jnp.arange(0, AXIS_ROT_DIM, 2, dtype=jnp.float32) / AXIS_ROT_DIM)
    ang = jnp.concatenate([row_idx[:, None] * inv_freq, col_idx[:, None] * inv_freq], axis=-1)
    return jnp.cos(ang), jnp.sin(ang)


def _apply_rope(x, cos, sin):
    shape = (1, x.shape[1]) + (1,) * (x.ndim - 3) + (HEAD_DIM // 2,)
    c = cos.reshape(shape)
    s = sin.reshape(shape)
    xf = x.astype(jnp.float32).reshape(x.shape[:-1] + (HEAD_DIM // 2, 2))
    x0 = xf[..., 0]
    x1 = xf[..., 1]
    out = jnp.stack([x0 * c - x1 * s, x0 * s + x1 * c], axis=-1).reshape(x.shape)
    return out.astype(x.dtype)


def _block_attention(q, k, v):
    bsz, s = q.shape[0], q.shape[1]
    nblk = s // Q_BLOCK
    qb = q.reshape(bsz, nblk, Q_BLOCK, N_KV_HEADS, GQA_GROUP, HEAD_DIM).swapaxes(0, 1)
    scale = HEAD_DIM ** -0.5

    def one_block(qblk):
        scores = jnp.einsum('bqkgd,bskd->bkgqs', qblk, k, preferred_element_type=jnp.float32) * scale
        probs = jax.nn.softmax(scores, axis=-1).astype(v.dtype)
        return jnp.einsum('bkgqs,bskd->bqkgd', probs, v)

    out = lax.map(one_block, qb)
    return out.swapaxes(0, 1).reshape(bsz, s, ATTN_DIM)


def _attn_mixer(q, k, v, q_norm_w, k_norm_w, cos, sin):
    bsz, s = q.shape[0], q.shape[1]
    q = q.reshape(bsz, s, N_KV_HEADS, GQA_GROUP, HEAD_DIM)
    k = k.reshape(bsz, s, N_KV_HEADS, HEAD_DIM)
    v = v.reshape(bsz, s, N_KV_HEADS, HEAD_DIM)
    q = _apply_rope(_rms_norm(q, q_norm_w), cos, sin)
    k = _apply_rope(_rms_norm(k, k_norm_w), cos, sin)
    return _block_attention(q, k, v)


def _ssd_scan(x, dt, a, b_mat, c_mat):
    bsz, s = x.shape[0], x.shape[1]
    nc = s // CHUNK
    xs = x.reshape(bsz, nc, CHUNK, N_SSM_GROUPS, HEADS_PER_GROUP, SSM_HEADDIM)
    dts = dt.reshape(bsz, nc, CHUNK, N_SSM_GROUPS, HEADS_PER_GROUP)
    bc = b_mat.reshape(bsz, nc, CHUNK, N_SSM_GROUPS, D_STATE)
    cc = c_mat.reshape(bsz, nc, CHUNK, N_SSM_GROUPS, D_STATE)
    xdt = xs * dts[..., None]
    a_cum = jnp.cumsum((dts * a).transpose(0, 3, 4, 1, 2), axis=-1)
    seg = a_cum[..., :, None] - a_cum[..., None, :]
    lower = jnp.tril(jnp.ones((CHUNK, CHUNK), dtype=bool))
    decay = jnp.where(lower, jnp.exp(jnp.where(lower, seg, 0.0)), 0.0)
    cb = jnp.einsum('bclgn,bcsgn->bgcls', cc, bc)
    y_diag = jnp.einsum('bgcls,bgrcls,bcsgrp->bclgrp', cb, decay, xdt)
    decay_states = jnp.exp(a_cum[..., -1:] - a_cum)
    states = jnp.einsum('bclgn,bgrcl,bclgrp->bcgrpn', bc, decay_states, xdt)
    chunk_decay = jnp.exp(a_cum[..., -1])

    def step(h, inp):
        st, dec = inp
        return h * dec[..., None, None] + st, h

    h0 = jnp.zeros((bsz, N_SSM_GROUPS, HEADS_PER_GROUP, SSM_HEADDIM, D_STATE), jnp.float32)
    _, h_in = lax.scan(step, h0, (jnp.moveaxis(states, 1, 0), jnp.moveaxis(chunk_decay, -1, 0)))
    y_off = jnp.einsum('bclgn,cbgrpn,bgrcl->bclgrp', cc, h_in, jnp.exp(a_cum))
    return (y_diag + y_off).reshape(bsz, s, N_SSM_GROUPS, HEADS_PER_GROUP, SSM_HEADDIM)


def _ssd_mixer(z, xbc, dt_raw, conv_w, conv_b, dt_bias, a_log, d_skip, norm_w):
    bsz, s = z.shape[0], z.shape[1]
    xbc = jax.nn.silu(_depthwise_conv(xbc, conv_w, conv_b))
    gn = N_SSM_GROUPS * D_STATE
    f32 = jnp.float32
    x5 = xbc[..., :D_INNER].astype(f32).reshape(bsz, s, N_SSM_GROUPS, HEADS_PER_GROUP, SSM_HEADDIM)
    b4 = xbc[..., D_INNER:D_INNER + gn].astype(f32).reshape(bsz, s, N_SSM_GROUPS, D_STATE)
    c4 = xbc[..., D_INNER + gn:].astype(f32).reshape(bsz, s, N_SSM_GROUPS, D_STATE)
    dt_all = jax.nn.softplus(dt_raw.astype(f32).reshape(bsz, s, 2, N_SSM_HEADS) + dt_bias.astype(f32))
    dt_all = dt_all.reshape(bsz, s, 2, N_SSM_GROUPS, HEADS_PER_GROUP)
    a = -jnp.exp(a_log.astype(f32)).reshape(2, N_SSM_GROUPS, HEADS_PER_GROUP)
    y_fwd = _ssd_scan(x5, dt_all[:, :, 0], a[0], b4, c4)
    flip = lambda t: jnp.flip(t, axis=1)
    y_bwd = flip(_ssd_scan(flip(x5), flip(dt_all[:, :, 1]), a[1], flip(b4), flip(c4)))
    y = y_fwd + y_bwd + x5 * d_skip.astype(f32).reshape(N_SSM_GROUPS, HEADS_PER_GROUP)[..., None]
    gsz = HEADS_PER_GROUP * SSM_HEADDIM
    y = y.reshape(bsz, s, N_SSM_GROUPS, gsz) * jax.nn.silu(z.astype(f32)).reshape(bsz, s, N_SSM_GROUPS, gsz)
    y = y * lax.rsqrt(jnp.mean(y * y, axis=-1, keepdims=True) + NORM_EPS) * norm_w.astype(f32).reshape(N_SSM_GROUPS, gsz)
    return y.reshape(bsz, s, D_INNER).astype(z.dtype)


def _conv_ffn(h, w_up, conv_w, conv_b, w_down):
    u = _depthwise_conv(h @ w_up, conv_w, conv_b)
    gate = u[..., :D_FF]
    val = u[..., D_FF:]
    return (jax.nn.gelu(gate, approximate=True) * val) @ w_down


def _trunk(x, p, norm1_w, w_in, ssm_conv_w, ssm_conv_b, dt_bias, a_log, d_skip, ssd_norm_w,
           q_norm_w, k_norm_w, gate_b, w_attn_branch, w_ssd_branch, w_out, norm2_w, w_up,
           ffn_conv_w, ffn_conv_b, w_down, w_ple, w_ple_gate, b_ple_gate, final_norm_w):
    cos, sin = _axial_rope(x.shape[1])
    for i in range(DEPTH):
        h = _rms_norm(x, norm1_w[i])
        q, k, v, z, xbc, dt_raw, gates = _split_columns(h @ w_in[i])
        attn = _attn_mixer(q, k, v, q_norm_w[i], k_norm_w[i], cos, sin)
        ssd = _ssd_mixer(z, xbc, dt_raw, ssm_conv_w[i], ssm_conv_b[i], dt_bias[i], a_log[i], d_skip[i], ssd_norm_w[i])
        g = jax.nn.sigmoid(gates + gate_b[i])
        merged = g[..., :D_MODEL] * (attn @ w_attn_branch[i]) + g[..., D_MODEL:] * (ssd @ w_ssd_branch[i])
        x = x + merged @ w_out[i]
        x = x + _conv_ffn(_rms_norm(x, norm2_w[i]), w_up[i], ffn_conv_w[i], ffn_conv_b[i], w_down[i])
        x = x + jax.nn.sigmoid(x @ w_ple_gate[i] + b_ple_gate[i]) * (p[i] @ w_ple[i])
    return _rms_norm(x, final_norm_w)


def setup_inputs(seed: int = 0) -> dict:
    key = jax.random.key(seed)
    ks = jax.random.split(key, 32)
    f32 = jnp.float32

    def nrm(k, shape, scale):
        return jax.random.normal(k, shape, f32) * scale

    dt0 = jnp.exp(jax.random.uniform(ks[10], (DEPTH, 2, N_SSM_HEADS), f32, math.log(1e-3), math.log(1e-1)))
    return {
        'x_prompt': nrm(ks[0], (BATCH, SEQ, D_MODEL), 1.0),
        'x_sample': nrm(ks[1], (DEC_BATCH, DEC_SEQ, D_MODEL), 1.0),
        'p_prompt': nrm(ks[2], (DEPTH, BATCH, SEQ, PLE_DIM), 1.0),
        'p_sample': nrm(ks[3], (DEPTH, DEC_BATCH, DEC_SEQ, PLE_DIM), 1.0),
        'norm1_w': 1.0 + nrm(ks[4], (DEPTH, D_MODEL), 0.05),
        'w_in': nrm(ks[5], (DEPTH, D_MODEL, IN_DIM), D_MODEL ** -0.5),
        'ssm_conv_w': nrm(ks[6], (DEPTH, SSM_CONV, CONV_DIM), SSM_CONV ** -0.5),
        'ssm_conv_b': nrm(ks[7], (DEPTH, CONV_DIM), 0.01),
        'dt_bias': dt0 + jnp.log(-jnp.expm1(-dt0)),
        'a_log': jnp.log(jax.random.uniform(ks[11], (DEPTH, 2, N_SSM_HEADS), f32, 1.0, 16.0)),
        'd_skip': 1.0 + nrm(ks[12], (DEPTH, N_SSM_HEADS), 0.1),
        'ssd_norm_w': 1.0 + nrm(ks[13], (DEPTH, D_INNER), 0.05),
        'q_norm_w': 1.0 + nrm(ks[14], (DEPTH, HEAD_DIM), 0.05),
        'k_norm_w': 1.0 + nrm(ks[15], (DEPTH, HEAD_DIM), 0.05),
        'gate_b': nrm(ks[16], (DEPTH, 2 * D_MODEL), 0.01),
        'w_attn_branch': nrm(ks[17], (DEPTH, ATTN_DIM, D_MODEL), ATTN_DIM ** -0.5),
        'w_ssd_branch': nrm(ks[18], (DEPTH, D_INNER, D_MODEL), D_INNER ** -0.5),
        'w_out': nrm(ks[19], (DEPTH, D_MODEL, D_MODEL), D_MODEL ** -0.5),
        'norm2_w': 1.0 + nrm(ks[20], (DEPTH, D_MODEL), 0.05),
        'w_up': nrm(ks[21], (DEPTH, D_MODEL, 2 * D_FF), D_MODEL ** -0.5),
        'ffn_conv_w': nrm(ks[22], (DEPTH, FFN_CONV, 2 * D_FF), FFN_CONV ** -0.5),
        'ffn_conv_b': nrm(ks[23], (DEPTH, 2 * D_FF), 0.01),
        'w_down': nrm(ks[24], (DEPTH, D_FF, D_MODEL), D_FF ** -0.5),
        'w_ple': nrm(ks[25], (DEPTH, PLE_DIM, D_MODEL), PLE_DIM ** -0.5),
        'w_ple_gate': nrm(ks[26], (DEPTH, D_MODEL, D_MODEL), D_MODEL ** -0.5),
        'b_ple_gate': nrm(ks[27], (DEPTH, D_MODEL), 0.01),
        'final_norm_w': 1.0 + nrm(ks[28], (D_MODEL,), 0.05),
    }


def reference(x_prompt, x_sample, p_prompt, p_sample, norm1_w, w_in, ssm_conv_w, ssm_conv_b, dt_bias,
              a_log, d_skip, ssd_norm_w, q_norm_w, k_norm_w, gate_b, w_attn_branch, w_ssd_branch, w_out,
              norm2_w, w_up, ffn_conv_w, ffn_conv_b, w_down, w_ple, w_ple_gate, b_ple_gate, final_norm_w):
    y_prompt = _trunk(x_prompt, p_prompt, norm1_w, w_in, ssm_conv_w, ssm_conv_b, dt_bias, a_log, d_skip,
                      ssd_norm_w, q_norm_w, k_norm_w, gate_b, w_attn_branch, w_ssd_branch, w_out, norm2_w,
                      w_up, ffn_conv_w, ffn_conv_b, w_down, w_ple, w_ple_gate, b_ple_gate, final_norm_w)
    y_sample = _trunk(x_sample, p_sample, norm1_w, w_in, ssm_conv_w, ssm_conv_b, dt_bias, a_log, d_skip,
                      ssd_norm_w, q_norm_w, k_norm_w, gate_b, w_attn_branch, w_ssd_branch, w_out, norm2_w,
                      w_up, ffn_conv_w, ffn_conv_b, w_down, w_ple, w_ple_gate, b_ple_gate, final_norm_w)
    return (y_prompt, y_sample)
```

```python
import functools
import math

import jax
import jax.numpy as jnp
from jax import lax
from jax.experimental import pallas as pl
from jax.experimental.pallas import tpu as pltpu

F32 = jnp.float32
BF16 = jnp.bfloat16

D_MODEL = 1024
N_HEADS = 16
N_KV_HEADS = 4
HEAD_DIM = 64
GQA_GROUP = N_HEADS // N_KV_HEADS
ATTN_DIM = N_HEADS * HEAD_DIM
KV_DIM = N_KV_HEADS * HEAD_DIM
GROUP_Q_DIM = GQA_GROUP * HEAD_DIM
AXIS_ROT_DIM = HEAD_DIM // 2
ROPE_THETA = 10000.0
GRID_W = 64

D_INNER = 2048
SSM_HEADDIM = 64
N_SSM_HEADS = D_INNER // SSM_HEADDIM
N_SSM_GROUPS = 4
HEADS_PER_GROUP = N_SSM_HEADS // N_SSM_GROUPS
D_STATE = 128
GROUP_INNER = HEADS_PER_GROUP * SSM_HEADDIM
CONV_DIM = D_INNER + 2 * N_SSM_GROUPS * D_STATE
CHUNK = 128

D_FF = 2816
PLE_DIM = 256
NORM_EPS = 1e-6

V7X_LANES = 128
V7X_BF16_SUBLANES = 16
V7X_VMEM_LIMIT_BYTES = 60000 * 1024

MASKED_SEG = -1e30
LOG2E = 1.4426950408889634

ROW_TILE = 256
KV_PREP_TILE = 512
ATTN_Q_TILE = 256
FFN_ROW_TILE = 256
HALO = V7X_BF16_SUBLANES


def _cparams(*sem):
    return pltpu.CompilerParams(dimension_semantics=sem, vmem_limit_bytes=V7X_VMEM_LIMIT_BYTES)


def _const_spec(shape):
    nd = len(shape)
    return pl.BlockSpec(shape, lambda *_: (0,) * nd)


def _dot(a, b):
    return jnp.dot(a, b, preferred_element_type=F32)


def _dot_nt(a, b):
    return lax.dot_general(a, b, (((1,), (1,)), ((), ())), preferred_element_type=F32)


def _rms_rows(x, w):
    return x * lax.rsqrt(jnp.mean(x * x, axis=-1, keepdims=True) + NORM_EPS) * w


def _sigmoid(x):
    return 1.0 / (1.0 + jnp.exp(-x))


_IN_PROJ_OUTS = (
    (ATTN_DIM, BF16), (KV_DIM, BF16), (KV_DIM, BF16), (D_INNER, BF16), (CONV_DIM, BF16),
    (2 * D_MODEL, BF16), (2 * N_SSM_HEADS, F32))
_N_CHUNK = 512


def _in_proj_kernel(x_ref, nw_ref, w_ref, *out_refs):
    h = _rms_rows(x_ref[...], nw_ref[...]).astype(BF16)
    col = 0
    for o_ref, (width, _) in zip(out_refs, _IN_PROJ_OUTS):
        for c in range(0, width, _N_CHUNK):
            cw = min(_N_CHUNK, width - c)
            o_ref[:, c:c + cw] = _dot(h, w_ref[:, col + c:col + c + cw]).astype(o_ref.dtype)
        col += width


def _in_proj(x2d, norm_w, w_all):
    t = x2d.shape[0]
    tm = min(ROW_TILE, t)
    n_all = w_all.shape[1]
    return pl.pallas_call(
        _in_proj_kernel,
        grid=(t // tm,),
        in_specs=[pl.BlockSpec((tm, D_MODEL), lambda i: (i, 0)),
                  _const_spec((1, D_MODEL)),
                  _const_spec((D_MODEL, n_all))],
        out_specs=[pl.BlockSpec((tm, w), lambda i: (i, 0)) for w, _ in _IN_PROJ_OUTS],
        out_shape=[jax.ShapeDtypeStruct((t, w), dt) for w, dt in _IN_PROJ_OUTS],
        compiler_params=_cparams("parallel"),
        name="in_proj",
    )(x2d, norm_w, w_all)


def _head_norm_rope(x, ones_blk, w, cos, sin_signed):
    width = x.shape[-1]
    ss = _dot((x * x).astype(BF16), ones_blk)
    xn = x * lax.rsqrt(ss * (1.0 / HEAD_DIM) + NORM_EPS) * w
    lane = lax.broadcasted_iota(jnp.int32, xn.shape, 1)
    half = HEAD_DIM // 2
    partner = jnp.where((lane % HEAD_DIM) < half,
                        pltpu.roll(xn, width - half, 1), pltpu.roll(xn, half, 1))
    return xn * cos + partner * sin_signed


def _kv_prep_kernel(k_ref, v_ref, cos_ref, sin_ref, ones_ref, kw_ref, ko_ref, vto_ref):
    k = _head_norm_rope(k_ref[0].astype(F32), ones_ref[...], kw_ref[...],
                        cos_ref[...], sin_ref[...])
    vt = v_ref[0].astype(F32).T
    for h in range(N_KV_HEADS):
        ko_ref[0, h] = k[:, h * HEAD_DIM:(h + 1) * HEAD_DIM].astype(BF16)
        vto_ref[0, h, 0] = vt[h * HEAD_DIM:(h + 1) * HEAD_DIM, :].astype(BF16)


def _kv_prep(k, v, cos, sin_signed, ones_blk, kw, ts):
    b, s, _ = k.shape
    nk = s // ts
    return pl.pallas_call(
        _kv_prep_kernel,
        grid=(b, nk),
        in_specs=[pl.BlockSpec((1, ts, KV_DIM), lambda i, j: (i, j, 0)),
                  pl.BlockSpec((1, ts, KV_DIM), lambda i, j: (i, j, 0)),
                  pl.BlockSpec((ts, KV_DIM), lambda i, j: (j, 0)),
                  pl.BlockSpec((ts, KV_DIM), lambda i, j: (j, 0)),
                  _const_spec((KV_DIM, KV_DIM)),
                  _const_spec((1, KV_DIM))],
        out_specs=[pl.BlockSpec((1, N_KV_HEADS, ts, HEAD_DIM), lambda i, j: (i, 0, j, 0)),
                   pl.BlockSpec((1, N_KV_HEADS, 1, HEAD_DIM, ts), lambda i, j: (i, 0, j, 0, 0))],
        out_shape=[jax.ShapeDtypeStruct((b, N_KV_HEADS, s, HEAD_DIM), BF16),
                   jax.ShapeDtypeStruct((b, N_KV_HEADS, nk, HEAD_DIM, ts), BF16)],
        compiler_params=_cparams("parallel", "parallel"),
        name="kv_prep",
    )(k, v, cos, sin_signed, ones_blk, kw)


def _attn_kernel(q_ref, k_ref, vt_ref, cos_ref, sin_ref, ones_ref, qw_ref, o_ref,
                 qs_ref, m_ref, l_ref, acc_ref, *, tq, tk, nk):
    q = _head_norm_rope(q_ref[0].astype(F32), ones_ref[...], qw_ref[...],
                        cos_ref[...], sin_ref[...])
    q = q * (HEAD_DIM ** -0.5 * LOG2E)
    for h in range(GQA_GROUP):
        qs_ref[h * tq:(h + 1) * tq, :] = q[:, h * HEAD_DIM:(h + 1) * HEAD_DIM].astype(BF16)
    m_ref[...] = jnp.full(m_ref.shape, -jnp.inf, F32)
    l_ref[...] = jnp.zeros(l_ref.shape, F32)
    acc_ref[...] = jnp.zeros(acc_ref.shape, F32)

    def kv_step(j, carry):
        k = k_ref[0, 0, pl.ds(pl.multiple_of(j * tk, tk), tk), :]
        s_t = _dot_nt(k, qs_ref[...])
        m_prev = m_ref[...]
        m_new = jnp.maximum(m_prev, jnp.max(s_t, axis=0, keepdims=True))
        p = jnp.exp2(s_t - m_new)
        alpha = jnp.exp2(m_prev - m_new)
        l_ref[...] = alpha * l_ref[...] + jnp.sum(p, axis=0, keepdims=True)
        acc_ref[...] = alpha * acc_ref[...] + _dot(vt_ref[0, 0, j], p.astype(BF16))
        m_ref[...] = m_new
        return carry

    lax.fori_loop(0, nk, kv_step, 0)
    out_t = acc_ref[...] * (1.0 / l_ref[...])
    out_t = jnp.concatenate([out_t[:, h * tq:(h + 1) * tq] for h in range(GQA_GROUP)], axis=0)
    o_ref[0] = out_t.T.astype(o_ref.dtype)


def _attention(q, k_heads, vt_heads, cos, sin_signed, ones_blk, qw, tq):
    b, s, _ = q.shape
    nk, tk = vt_heads.shape[2], vt_heads.shape[4]
    vq = GQA_GROUP * tq
    kern = functools.partial(_attn_kernel, tq=tq, tk=tk, nk=nk)
    return pl.pallas_call(
        kern,
        grid=(b, N_KV_HEADS, s // tq),
        in_specs=[pl.BlockSpec((1, tq, GROUP_Q_DIM), lambda i, g, j: (i, j, g)),
                  pl.BlockSpec((1, 1, s, HEAD_DIM), lambda i, g, j: (i, g, 0, 0)),
                  pl.BlockSpec((1, 1, nk, HEAD_DIM, tk), lambda i, g, j: (i, g, 0, 0, 0)),
                  pl.BlockSpec((tq, GROUP_Q_DIM), lambda i, g, j: (j, 0)),
                  pl.BlockSpec((tq, GROUP_Q_DIM), lambda i, g, j: (j, 0)),
                  _const_spec((GROUP_Q_DIM, GROUP_Q_DIM)),
                  _const_spec((1, GROUP_Q_DIM))],
        out_specs=pl.BlockSpec((1, tq, GROUP_Q_DIM), lambda i, g, j: (i, j, g)),
        out_shape=jax.ShapeDtypeStruct((b, s, ATTN_DIM), BF16),
        scratch_shapes=[pltpu.VMEM((vq, HEAD_DIM), BF16),
                        pltpu.VMEM((1, vq), F32),
                        pltpu.VMEM((1, vq), F32),
                        pltpu.VMEM((HEAD_DIM, vq), F32)],
        compiler_params=_cparams("parallel", "parallel", "arbitrary"),
        name="attention",
    )(q, k_heads, vt_heads, cos, sin_signed, ones_blk, qw)


def _split3(v):
    hi = v.astype(BF16)
    r1 = v - hi.astype(F32)
    mid = r1.astype(BF16)
    lo = (r1 - mid.astype(F32)).astype(BF16)
    return hi, mid, lo


def _expand_heads(v, e3_ref):
    parts = jnp.concatenate(_split3(v), axis=1)
    return _dot(parts, e3_ref[...])


def _conv_silu(main_ref, prev_ref, next_ref, cw_ref, cb_ref, xs_ref, prev_ok, next_ok, rows):
    base = 8
    xs_ref[base:base + rows, :] = main_ref[0].astype(F32)
    xs_ref[base - 1:base, :] = prev_ref[0, HALO - 1:HALO, :].astype(F32) * prev_ok
    xs_ref[base + rows:base + rows + 1, :] = next_ref[0, 0:1, :].astype(F32) * next_ok
    conv = (xs_ref[base - 1:base - 1 + rows, :] * cw_ref[0:1, :]
            + xs_ref[base:base + rows, :] * cw_ref[1:2, :]
            + xs_ref[base + 1:base + 1 + rows, :] * cw_ref[2:3, :] + cb_ref[...])
    return conv * _sigmoid(conv)


def _ssd_kernel(*refs, reverse, nc):
    if reverse:
        (xbc_ref, prev_ref, next_ref, dt_ref, z_ref, yf_ref, cw_ref, cb_ref, dtb_ref, a_ref,
         tri_ref, e3_ref, dskip_ref, nw_ref, o_ref, xs_ref, h_ref) = refs
    else:
        (xbc_ref, prev_ref, next_ref, dt_ref, cw_ref, cb_ref, dtb_ref, a_ref,
         tri_ref, e3_ref, o_ref, xs_ref, h_ref) = refs
    L = CHUNK
    c = pl.program_id(1)
    cc = nc - 1 - c if reverse else c

    @pl.when(c == 0)
    def _():
        h_ref[...] = jnp.zeros(h_ref.shape, F32)

    prev_ok = (cc > 0).astype(F32)
    next_ok = (cc < nc - 1).astype(F32)
    xc = _conv_silu(xbc_ref, prev_ref, next_ref, cw_ref, cb_ref, xs_ref, prev_ok, next_ok, L)
    x = xc[:, :D_INNER]

    d0 = N_SSM_HEADS if reverse else 0
    dt_raw = dt_ref[0][:, d0:d0 + N_SSM_HEADS] + dtb_ref[...]
    dtv = jnp.maximum(dt_raw, 0.0) + jnp.log(1.0 + jnp.exp(-jnp.abs(dt_raw)))
    da = dtv * a_ref[...]

    tri = tri_ref[...]
    da3 = jnp.concatenate(_split3(da), axis=1)
    cum3 = _dot(tri, da3)
    nh = N_SSM_HEADS
    cum = cum3[:, :nh] + cum3[:, nh:2 * nh] + cum3[:, 2 * nh:3 * nh]
    cum3_t = lax.dot_general(da3, tri, (((0,), (1,)), ((), ())), preferred_element_type=F32)
    cum_t = cum3_t[:nh] + cum3_t[nh:2 * nh] + cum3_t[2 * nh:3 * nh]
    last = 0 if reverse else L - 1
    tot = cum[last:last + 1, :]

    dt_e = _expand_heads(dtv, e3_ref)
    grow_e = _expand_heads(jnp.exp(cum), e3_ref)
    tail_e = _expand_heads(dtv * jnp.exp(tot - cum), e3_ref)
    xdt = (x * dt_e).astype(BF16)
    xtail = (x * tail_e).astype(BF16)
    chunk_decay = grow_e[last:last + 1, :]

    li = lax.broadcasted_iota(jnp.int32, (L, L), 0)
    si = lax.broadcasted_iota(jnp.int32, (L, L), 1)
    mask = (si >= li) if reverse else (si <= li)
    lane = lax.broadcasted_iota(jnp.int32, (L, 2 * SSM_HEADDIM), 1)
    first_head = lane < SSM_HEADDIM

    y_parts = []
    for g in range(N_SSM_GROUPS):
        b_g = xc[:, D_INNER + g * D_STATE:D_INNER + (g + 1) * D_STATE]
        c_g = xc[:, D_INNER + (N_SSM_GROUPS + g) * D_STATE:D_INNER + (N_SSM_GROUPS + g + 1) * D_STATE]
        b_bf = b_g.astype(BF16)
        c_bf = c_g.astype(BF16)
        cb = _dot_nt(c_bf, b_bf)
        gs = slice(g * GROUP_INNER, (g + 1) * GROUP_INNER)
        h_in = h_ref[g]
        y_off = _dot(c_bf, h_in.astype(BF16)) * grow_e[:, gs]
        h_ref[g] = h_in * chunk_decay[:, gs] + _dot(b_g.T.astype(BF16), xtail[:, gs])
        diag = []
        for pr in range(HEADS_PER_GROUP // 2):
            ms = []
            for hh in (g * HEADS_PER_GROUP + 2 * pr, g * HEADS_PER_GROUP + 2 * pr + 1):
                seg = cum[:, hh:hh + 1] - cum_t[hh:hh + 1, :]
                dec = jnp.exp(jnp.where(mask, seg, MASKED_SEG))
                ms.append((cb * dec).astype(BF16))
            lhs = jnp.concatenate(ms, axis=1)
            xp = xdt[:, g * GROUP_INNER + pr * 128:g * GROUP_INNER + (pr + 1) * 128]
            zero = jnp.zeros_like(xp)
            rhs = jnp.concatenate([jnp.where(first_head, xp, zero),
                                   jnp.where(first_head, zero, xp)], axis=0)
            diag.append(_dot(lhs, rhs))
        y_parts.append(jnp.concatenate(diag, axis=1) + y_off)
    y = jnp.concatenate(y_parts, axis=1)

    if not reverse:
        o_ref[0] = y.astype(o_ref.dtype)
    else:
        y = y + yf_ref[0].astype(F32) + x * dskip_ref[...]
        zf = z_ref[0].astype(F32)
        y = y * (zf * _sigmoid(zf))
        outs = []
        for g in range(N_SSM_GROUPS):
            yg = y[:, g * GROUP_INNER:(g + 1) * GROUP_INNER]
            outs.append(yg * lax.rsqrt(jnp.mean(yg * yg, axis=-1, keepdims=True) + NORM_EPS))
        o_ref[0] = (jnp.concatenate(outs, axis=1) * nw_ref[...]).astype(o_ref.dtype)


def _ssd_pass(xbc, dt, consts, reverse, extra=()):
    b, s, _ = xbc.shape
    nc = s // CHUNK
    nhalo = s // HALO
    per = CHUNK // HALO

    def pos(c):
        return nc - 1 - c if reverse else c

    chunk_map = lambda i, c: (i, pos(c), 0)
    prev_map = lambda i, c: (i, jnp.maximum(pos(c) * per - 1, 0), 0)
    next_map = lambda i, c: (i, jnp.minimum((pos(c) + 1) * per, nhalo - 1), 0)
    in_specs = [pl.BlockSpec((1, CHUNK, CONV_DIM), chunk_map),
                pl.BlockSpec((1, HALO, CONV_DIM), prev_map),
                pl.BlockSpec((1, HALO, CONV_DIM), next_map),
                pl.BlockSpec((1, CHUNK, 2 * N_SSM_HEADS), chunk_map)]
    args = [xbc, xbc, xbc, dt]
    for e in extra:
        in_specs.append(pl.BlockSpec((1, CHUNK, D_INNER), chunk_map))
        args.append(e)
    for cst in consts:
        in_specs.append(_const_spec(cst.shape))
        args.append(cst)
    kern = functools.partial(_ssd_kernel, reverse=reverse, nc=nc)
    return pl.pallas_call(
        kern,
        grid=(b, nc),
        in_specs=in_specs,
        out_specs=pl.BlockSpec((1, CHUNK, D_INNER), chunk_map),
        out_shape=jax.ShapeDtypeStruct((b, s, D_INNER), BF16),
        scratch_shapes=[pltpu.VMEM((CHUNK + 16, CONV_DIM), F32),
                        pltpu.VMEM((N_SSM_GROUPS, D_STATE, GROUP_INNER), F32)],
        compiler_params=_cparams("parallel", "arbitrary"),
        name="ssd_bwd" if reverse else "ssd_fwd",
    )(*args)


def _merge_kernel(x_ref, attn_ref, ssd_ref, gate_ref, gb_ref, wa_ref, ws_ref, wo_ref, nw_ref,
                  x1_ref, h2_ref):
    g = _sigmoid(gate_ref[...].astype(F32) + gb_ref[...])
    merged = (g[:, :D_MODEL] * _dot(attn_ref[...], wa_ref[...])
              + g[:, D_MODEL:] * _dot(ssd_ref[...], ws_ref[...]))
    x1 = x_ref[...] + _dot(merged.astype(BF16), wo_ref[...])
    x1_ref[...] = x1
    h2_ref[...] = _rms_rows(x1, nw_ref[...]).astype(BF16)


def _merge(x2d, attn, ssd, gates, gate_b, wa, ws, wo, norm2_w):
    t = x2d.shape[0]
    tm = min(ROW_TILE, t)
    row = lambda w: pl.BlockSpec((tm, w), lambda i: (i, 0))
    return pl.pallas_call(
        _merge_kernel,
        grid=(t // tm,),
        in_specs=[row(D_MODEL), row(ATTN_DIM), row(D_INNER), row(2 * D_MODEL),
                  _const_spec(gate_b.shape), _const_spec(wa.shape), _const_spec(ws.shape),
                  _const_spec(wo.shape), _const_spec(norm2_w.shape)],
        out_specs=[row(D_MODEL), row(D_MODEL)],
        out_shape=[jax.ShapeDtypeStruct((t, D_MODEL), F32),
                   jax.ShapeDtypeStruct((t, D_MODEL), BF16)],
        compiler_params=_cparams("parallel"),
        name="merge",
    )(x2d, attn, ssd, gates, gate_b, wa, ws, wo, norm2_w)


def _ffn_up_kernel(h_ref, w_ref, u_ref):
    h = h_ref[...]
    n = w_ref.shape[1]
    for c in range(0, n, _N_CHUNK):
        u_ref[:, c:c + _N_CHUNK] = _dot(h, w_ref[:, c:c + _N_CHUNK]).astype(u_ref.dtype)


def _ffn_up(h2, w_up):
    t = h2.shape[0]
    tm = min(ROW_TILE, t)
    n = w_up.shape[1]
    return pl.pallas_call(
        _ffn_up_kernel,
        grid=(t // tm,),
        in_specs=[pl.BlockSpec((tm, D_MODEL), lambda i: (i, 0)), _const_spec(w_up.shape)],
        out_specs=pl.BlockSpec((tm, n), lambda i: (i, 0)),
        out_shape=jax.ShapeDtypeStruct((t, n), BF16),
        compiler_params=_cparams("parallel"),
        name="ffn_up",
    )(h2, w_up)


def _gelu_tanh(x):
    return 0.5 * x * (1.0 + jnp.tanh(math.sqrt(2.0 / math.pi) * (x + 0.044715 * (x * x * x))))


def _ffn_down_kernel(u_ref, prev_ref, next_ref, x1_ref, p_ref, cw_ref, cb_ref, wd_ref, wple_ref,
                     wpg_ref, bpg_ref, fw_ref, o_ref, us_ref, *, tm, tiles_per_seq):
    i = pl.program_id(0)
    j = i % tiles_per_seq
    prev_ok = (j > 0).astype(F32)
    next_ok = (j < tiles_per_seq - 1).astype(F32)
    base = 8
    us_ref[base:base + tm, :] = u_ref[...].astype(F32)
    us_ref[base - 1:base, :] = prev_ref[HALO - 1:HALO, :].astype(F32) * prev_ok
    us_ref[base + tm:base + tm + 1, :] = next_ref[0:1, :].astype(F32) * next_ok

    def conv(cols):
        return (us_ref[base - 1:base - 1 + tm, cols] * cw_ref[0:1, cols]
                + us_ref[base:base + tm, cols] * cw_ref[1:2, cols]
                + us_ref[base + 1:base + 1 + tm, cols] * cw_ref[2:3, cols] + cb_ref[:, cols])

    act = (_gelu_tanh(conv(slice(0, D_FF))) * conv(slice(D_FF, 2 * D_FF))).astype(BF16)
    x2 = x1_ref[...] + _dot(act, wd_ref[...])
    pg = _sigmoid(_dot(x2.astype(BF16), wpg_ref[...]) + bpg_ref[...])
    x3 = x2 + pg * _dot(p_ref[...].astype(BF16), wple_ref[...])
    o_ref[...] = _rms_rows(x3, fw_ref[...])


def _ffn_down(u, x1, p2d, seq, conv_w, conv_b, w_down, w_ple, w_pg, b_pg, final_w):
    t = u.shape[0]
    tm = min(FFN_ROW_TILE, seq)
    tiles_per_seq = seq // tm
    per = tm // HALO
    nhalo = t // HALO
    n = u.shape[1]
    kern = functools.partial(_ffn_down_kernel, tm=tm, tiles_per_seq=tiles_per_seq)
    consts = (conv_w, conv_b, w_down, w_ple, w_pg, b_pg, final_w)
    return pl.pallas_call(
        kern,
        grid=(t // tm,),
        in_specs=[pl.BlockSpec((tm, n), lambda i: (i, 0)),
                  pl.BlockSpec((HALO, n), lambda i: (jnp.maximum(i * per - 1, 0), 0)),
                  pl.BlockSpec((HALO, n), lambda i: (jnp.minimum((i + 1) * per, nhalo - 1), 0)),
                  pl.BlockSpec((tm, D_MODEL), lambda i: (i, 0)),
                  pl.BlockSpec((tm, PLE_DIM), lambda i: (i, 0))]
                 + [_const_spec(c.shape) for c in consts],
        out_specs=pl.BlockSpec((tm, D_MODEL), lambda i: (i, 0)),
        out_shape=jax.ShapeDtypeStruct((t, D_MODEL), F32),
        scratch_shapes=[pltpu.VMEM((tm + 16, n), F32)],
        compiler_params=_cparams("parallel"),
        name="ffn_down",
    )(u, u, u, x1, p2d, *consts)


def _rope_tables(n_tokens):
    rows = n_tokens // GRID_W
    row_idx = jnp.repeat(jnp.arange(rows, dtype=F32), GRID_W)
    col_idx = jnp.tile(jnp.arange(GRID_W, dtype=F32), rows)
    inv_freq = ROPE_THETA ** (-jnp.arange(0, AXIS_ROT_DIM, 2, dtype=F32) / AXIS_ROT_DIM)
    ang = jnp.concatenate([row_idx[:, None] * inv_freq, col_idx[:, None] * inv_freq], axis=-1)
    cos, sin = jnp.cos(ang), jnp.sin(ang)
    cos_h = jnp.concatenate([cos, cos], axis=-1)
    sin_h = jnp.concatenate([-sin, sin], axis=-1)
    return jnp.tile(cos_h, (1, N_KV_HEADS)), jnp.tile(sin_h, (1, N_KV_HEADS))


def _prepare(norm1_w, w_in, ssm_conv_w, ssm_conv_b, dt_bias, a_log, d_skip, ssd_norm_w,
             q_norm_w, k_norm_w, gate_b, w_attn_branch, w_ssd_branch, w_out, norm2_w, w_up,
             ffn_conv_w, ffn_conv_b, w_down, w_ple, w_ple_gate, b_ple_gate, final_norm_w):
    perm = jnp.concatenate([jnp.arange(0, HEAD_DIM, 2), jnp.arange(1, HEAD_DIM, 2)])
    w = w_in[0]
    sizes = (ATTN_DIM, KV_DIM, KV_DIM, D_INNER, CONV_DIM, 2 * N_SSM_HEADS, 2 * D_MODEL)
    parts, start = [], 0
    for sz in sizes:
        parts.append(w[:, start:start + sz])
        start += sz
    wq, wk, wv, wz, wxbc, wdt, wg = parts
    wq = wq.reshape(D_MODEL, N_HEADS, HEAD_DIM)[:, :, perm].reshape(D_MODEL, ATTN_DIM)
    wk = wk.reshape(D_MODEL, N_KV_HEADS, HEAD_DIM)[:, :, perm].reshape(D_MODEL, KV_DIM)
    w_all = jnp.concatenate([wq, wk, wv, wz, wxbc, wg, wdt], axis=1).astype(BF16)

    head_id = jnp.arange(KV_DIM) // HEAD_DIM
    ones_blk = (head_id[:, None] == head_id[None, :]).astype(BF16)
    li = jnp.arange(CHUNK)
    tri_fwd = (li[None, :] <= li[:, None]).astype(BF16)
    tri_bwd = (li[None, :] >= li[:, None]).astype(BF16)
    chan_head = jnp.arange(D_INNER) // SSM_HEADDIM
    e1 = (jnp.arange(N_SSM_HEADS)[:, None] == chan_head[None, :]).astype(BF16)
    a = -jnp.exp(a_log[0].astype(F32))
    return dict(
        norm1_w=norm1_w[0][None, :], w_all=w_all,
        ones_blk=ones_blk,
        qw=jnp.tile(q_norm_w[0][perm], GQA_GROUP)[None, :],
        kw=jnp.tile(k_norm_w[0][perm], N_KV_HEADS)[None, :],
        conv_w=ssm_conv_w[0], conv_b=ssm_conv_b[0][None, :],
        dtb_f=dt_bias[0][0][None, :], dtb_b=dt_bias[0][1][None, :],
        a_f=a[0][None, :], a_b=a[1][None, :],
        tri_f=tri_fwd, tri_b=tri_bwd, e3=jnp.tile(e1, (3, 1)),
        dskip=jnp.repeat(d_skip[0], SSM_HEADDIM)[None, :], ssd_nw=ssd_norm_w[0][None, :],
        gate_b=gate_b[0][None, :], wa=w_attn_branch[0].astype(BF16),
        ws=w_ssd_branch[0].astype(BF16), wo=w_out[0].astype(BF16), norm2_w=norm2_w[0][None, :],
        w_up=w_up[0].astype(BF16), ffn_cw=ffn_conv_w[0], ffn_cb=ffn_conv_b[0][None, :],
        w_down=w_down[0].astype(BF16), w_ple=w_ple[0].astype(BF16),
        w_pg=w_ple_gate[0].astype(BF16), b_pg=b_ple_gate[0][None, :],
        final_w=final_norm_w[None, :])


def _trunk(x, p, prm):
    b, s, _ = x.shape
    t = b * s
    x2d = x.reshape(t, D_MODEL)
    q, k, v, z, xbc, gates, dt = _in_proj(x2d, prm["norm1_w"], prm["w_all"])
    cos, sin_signed = _rope_tables(s)

    ts = min(KV_PREP_TILE, s)
    k_heads, vt_heads = _kv_prep(k.reshape(b, s, KV_DIM), v.reshape(b, s, KV_DIM), cos, sin_signed,
                                 prm["ones_blk"], prm["kw"], ts)
    attn = _attention(q.reshape(b, s, ATTN_DIM), k_heads, vt_heads, cos, sin_signed,
                      prm["ones_blk"], prm["qw"], min(ATTN_Q_TILE, s))

    xbc3 = xbc.reshape(b, s, CONV_DIM)
    dt3 = dt.reshape(b, s, 2 * N_SSM_HEADS)
    y_fwd = _ssd_pass(xbc3, dt3, (prm["conv_w"], prm["conv_b"], prm["dtb_f"], prm["a_f"],
                                  prm["tri_f"], prm["e3"]), reverse=False)
    ssd = _ssd_pass(xbc3, dt3, (prm["conv_w"], prm["conv_b"], prm["dtb_b"], prm["a_b"],
                                prm["tri_b"], prm["e3"], prm["dskip"], prm["ssd_nw"]),
                    reverse=True, extra=(z.reshape(b, s, D_INNER), y_fwd))

    x1, h2 = _merge(x2d, attn.reshape(t, ATTN_DIM), ssd.reshape(t, D_INNER), gates,
                    prm["gate_b"], prm["wa"], prm["ws"], prm["wo"], prm["norm2_w"])
    u = _ffn_up(h2, prm["w_up"])
    out = _ffn_down(u, x1, p[0].reshape(t, PLE_DIM), s, prm["ffn_cw"], prm["ffn_cb"],
                    prm["w_down"], prm["w_ple"], prm["w_pg"], prm["b_pg"], prm["final_w"])
    return out.reshape(b, s, D_MODEL)


def kernel(x_prompt, x_sample, p_prompt, p_sample, norm1_w, w_in, ssm_conv_w, ssm_conv_b, dt_bias,
           a_log, d_skip, ssd_norm_w, q_norm_w, k_norm_w, gate_b, w_attn_branch, w_ssd_branch,
           w_out, norm2_w, w_up, ffn_conv_w, ffn_conv_b, w_down, w_ple, w_ple_gate, b_ple_gate,
           final_norm_w):
    prm = _prepare(norm1_w, w_in, ssm_conv_w, ssm_conv_b, dt_bias, a_log, d_skip, ssd_norm_w,
                   q_norm_w, k_norm_w, gate_b, w_attn_branch, w_ssd_branch, w_out, norm2_w, w_up,
                   ffn_conv_w, ffn_conv_b, w_down, w_ple, w_ple_gate, b_ple_gate, final_norm_w)
    return (_trunk(x_prompt, p_prompt, prm), _trunk(x_sample, p_sample, prm))
```

```python
import functools
import math

import jax
import jax.numpy as jnp
from jax import lax
from jax.experimental import pallas as pl
from jax.experimental.pallas import tpu as pltpu

F32 = jnp.float32
BF16 = jnp.bfloat16

D_MODEL = 1024
N_HEADS = 16
N_KV_HEADS = 4
HEAD_DIM = 64
GQA_GROUP = N_HEADS // N_KV_HEADS
ATTN_DIM = N_HEADS * HEAD_DIM
KV_DIM = N_KV_HEADS * HEAD_DIM
GROUP_Q_DIM = GQA_GROUP * HEAD_DIM
VT_ROWS = HEAD_DIM + 16
AXIS_ROT_DIM = HEAD_DIM // 2
ROPE_THETA = 10000.0
GRID_W = 64

D_INNER = 2048
SSM_HEADDIM = 64
N_SSM_HEADS = D_INNER // SSM_HEADDIM
N_SSM_GROUPS = 4
HEADS_PER_GROUP = N_SSM_HEADS // N_SSM_GROUPS
D_STATE = 128
GROUP_INNER = HEADS_PER_GROUP * SSM_HEADDIM
CONV_DIM = D_INNER + 2 * N_SSM_GROUPS * D_STATE
CHUNK = 128

D_FF = 2816
PLE_DIM = 256
NORM_EPS = 1e-6

V7X_LANES = 128
V7X_BF16_SUBLANES = 16
V7X_VMEM_LIMIT_BYTES = 60000 * 1024

MASKED_SEG = -1e30
LOG2E = 1.4426950408889634

ROW_TILE = 256
KV_PREP_TILE = 1024
ATTN_Q_TILE = 256
ATTN_KEY_SUB = 256
ATTN_Q_SLAB = 256
ATTN_PIPE_DEPTH = 4
FFN_ROW_TILE = 256
HALO = V7X_BF16_SUBLANES


def _cparams(*sem):
    return pltpu.CompilerParams(dimension_semantics=sem, vmem_limit_bytes=V7X_VMEM_LIMIT_BYTES)


def _const_spec(shape):
    nd = len(shape)
    return pl.BlockSpec(shape, lambda *_: (0,) * nd)


def _dot(a, b):
    return jnp.dot(a, b, preferred_element_type=F32)


def _dot_nt(a, b):
    return lax.dot_general(a, b, (((1,), (1,)), ((), ())), preferred_element_type=F32)


def _rms_rows(x, w):
    return x * lax.rsqrt(jnp.mean(x * x, axis=-1, keepdims=True) + NORM_EPS) * w


def _sigmoid(x):
    return 1.0 / (1.0 + jnp.exp(-x))


_IN_PROJ_OUTS = (
    (ATTN_DIM, BF16), (KV_DIM, BF16), (KV_DIM, BF16), (D_INNER, BF16), (CONV_DIM, BF16),
    (2 * D_MODEL, BF16), (2 * N_SSM_HEADS, F32))
_N_CHUNK = 512


def _in_proj_kernel(x_ref, nw_ref, w_ref, *out_refs):
    h = _rms_rows(x_ref[...], nw_ref[...]).astype(BF16)
    col = 0
    for o_ref, (width, _) in zip(out_refs, _IN_PROJ_OUTS):
        for c in range(0, width, _N_CHUNK):
            cw = min(_N_CHUNK, width - c)
            o_ref[:, c:c + cw] = _dot(h, w_ref[:, col + c:col + c + cw]).astype(o_ref.dtype)
        col += width


def _in_proj(x2d, norm_w, w_all):
    t = x2d.shape[0]
    tm = min(ROW_TILE, t)
    n_all = w_all.shape[1]
    return pl.pallas_call(
        _in_proj_kernel,
        grid=(t // tm,),
        in_specs=[pl.BlockSpec((tm, D_MODEL), lambda i: (i, 0)),
                  _const_spec((1, D_MODEL)),
                  _const_spec((D_MODEL, n_all))],
        out_specs=[pl.BlockSpec((tm, w), lambda i: (i, 0)) for w, _ in _IN_PROJ_OUTS],
        out_shape=[jax.ShapeDtypeStruct((t, w), dt) for w, dt in _IN_PROJ_OUTS],
        compiler_params=_cparams("parallel"),
        name="in_proj",
    )(x2d, norm_w, w_all)


def _head_norm_rope(x, ones_blk, w, cos, sin_signed):
    width = x.shape[-1]
    ss = _dot((x * x).astype(BF16), ones_blk)
    xn = x * lax.rsqrt(ss * (1.0 / HEAD_DIM) + NORM_EPS) * w
    lane = lax.broadcasted_iota(jnp.int32, xn.shape, 1)
    half = HEAD_DIM // 2
    partner = jnp.where((lane % HEAD_DIM) < half,
                        pltpu.roll(xn, width - half, 1), pltpu.roll(xn, half, 1))
    return xn * cos + partner * sin_signed


def _kv_prep_kernel(k_ref, v_ref, cos_ref, sin_ref, ones_ref, kw_ref, ko_ref, vto_ref):
    k = _head_norm_rope(k_ref[0].astype(F32), ones_ref[...], kw_ref[...],
                        cos_ref[...], sin_ref[...])
    vt = v_ref[0].astype(F32).T
    ts = vt.shape[1]
    sub = lax.broadcasted_iota(jnp.int32, (VT_ROWS - HEAD_DIM, ts), 0)
    tail = jnp.where(sub == 0, 1.0, 0.0).astype(BF16)
    for h in range(N_KV_HEADS):
        ko_ref[0, h] = k[:, h * HEAD_DIM:(h + 1) * HEAD_DIM].astype(BF16)
        vto_ref[0, h, 0, 0:HEAD_DIM, :] = vt[h * HEAD_DIM:(h + 1) * HEAD_DIM, :].astype(BF16)
        vto_ref[0, h, 0, HEAD_DIM:VT_ROWS, :] = tail


def _kv_prep(k, v, cos, sin_signed, ones_blk, kw, ts):
    b, s, _ = k.shape
    nk = s // ts
    return pl.pallas_call(
        _kv_prep_kernel,
        grid=(b, nk),
        in_specs=[pl.BlockSpec((1, ts, KV_DIM), lambda i, j: (i, j, 0)),
                  pl.BlockSpec((1, ts, KV_DIM), lambda i, j: (i, j, 0)),
                  pl.BlockSpec((ts, KV_DIM), lambda i, j: (j, 0)),
                  pl.BlockSpec((ts, KV_DIM), lambda i, j: (j, 0)),
                  _const_spec((KV_DIM, KV_DIM)),
                  _const_spec((1, KV_DIM))],
        out_specs=[pl.BlockSpec((1, N_KV_HEADS, ts, HEAD_DIM), lambda i, j: (i, 0, j, 0)),
                   pl.BlockSpec((1, N_KV_HEADS, 1, VT_ROWS, ts), lambda i, j: (i, 0, j, 0, 0))],
        out_shape=[jax.ShapeDtypeStruct((b, N_KV_HEADS, s, HEAD_DIM), BF16),
                   jax.ShapeDtypeStruct((b, N_KV_HEADS, nk, VT_ROWS, ts), BF16)],
        compiler_params=_cparams("parallel", "parallel"),
        name="kv_prep",
    )(k, v, cos, sin_signed, ones_blk, kw)


def _attn_kernel(q_ref, k_ref, vt_ref, cos_ref, sin_ref, ones_ref, qw_ref, o_ref,
                 qs_ref, m_ref, acc_ref, sp_ref, *, tq, tk, nk):
    q = _head_norm_rope(q_ref[0].astype(F32), ones_ref[...], qw_ref[...],
                        cos_ref[...], sin_ref[...])
    q = q * (HEAD_DIM ** -0.5 * LOG2E)
    for h in range(GQA_GROUP):
        qs_ref[h * tq:(h + 1) * tq, :] = q[:, h * HEAD_DIM:(h + 1) * HEAD_DIM].astype(BF16)
    m_ref[...] = jnp.full(m_ref.shape, -jnp.inf, F32)
    acc_ref[...] = jnp.zeros(acc_ref.shape, F32)

    units = [(i, n) for i in range(tk // ATTN_KEY_SUB)
             for n in range(GQA_GROUP * tq // ATTN_Q_SLAB)]
    n_units = len(units)
    depth = min(ATTN_PIPE_DEPTH, n_units)

    def scores(j, unit):
        i, n = unit
        row0 = pl.multiple_of(j * tk + i * ATTN_KEY_SUB, ATTN_KEY_SUB)
        k = k_ref[0, 0, pl.ds(row0, ATTN_KEY_SUB), :]
        return _dot_nt(k, qs_ref[n * ATTN_Q_SLAB:(n + 1) * ATTN_Q_SLAB, :])

    def softmax_pv(j, unit, s_t):
        i, n = unit
        cs = slice(n * ATTN_Q_SLAB, (n + 1) * ATTN_Q_SLAB)
        v_i = vt_ref[0, 0, j, :, i * ATTN_KEY_SUB:(i + 1) * ATTN_KEY_SUB]
        m_prev = m_ref[:, cs]
        m_new = jnp.maximum(m_prev, jnp.max(s_t, axis=0, keepdims=True))
        p = jnp.exp2(s_t - m_new)
        alpha = jnp.exp2(m_prev - m_new)
        acc_ref[:, cs] = alpha * acc_ref[:, cs] + _dot(v_i, p.astype(BF16))
        m_ref[:, cs] = m_new

    for d in range(depth):
        sp_ref[d] = scores(0, units[d])

    def kv_step(j, carry):
        j_next = jnp.minimum(j + 1, nk - 1)
        pending = {}
        for idx, u in enumerate(units):
            ahead = idx + depth
            if ahead < n_units:
                pending[ahead] = scores(j, units[ahead])
            s_t = sp_ref[idx] if idx < depth else pending.pop(idx)
            softmax_pv(j, u, s_t)
            if ahead >= n_units:
                sp_ref[ahead - n_units] = scores(j_next, units[ahead - n_units])
        return carry

    lax.fori_loop(0, nk, kv_step, 0)
    out_t = acc_ref[0:HEAD_DIM, :] * (1.0 / acc_ref[HEAD_DIM:HEAD_DIM + 1, :])
    out_t = jnp.concatenate([out_t[:, h * tq:(h + 1) * tq] for h in range(GQA_GROUP)], axis=0)
    o_ref[0] = out_t.T.astype(o_ref.dtype)


def _attention(q, k_heads, vt_heads, cos, sin_signed, ones_blk, qw, tq):
    b, s, _ = q.shape
    nk, tk = vt_heads.shape[2], vt_heads.shape[4]
    vq = GQA_GROUP * tq
    kern = functools.partial(_attn_kernel, tq=tq, tk=tk, nk=nk)
    return pl.pallas_call(
        kern,
        grid=(b, N_KV_HEADS, s // tq),
        in_specs=[pl.BlockSpec((1, tq, GROUP_Q_DIM), lambda i, g, j: (i, j, g)),
                  pl.BlockSpec((1, 1, s, HEAD_DIM), lambda i, g, j: (i, g, 0, 0)),
                  pl.BlockSpec((1, 1, nk, VT_ROWS, tk), lambda i, g, j: (i, g, 0, 0, 0)),
                  pl.BlockSpec((tq, GROUP_Q_DIM), lambda i, g, j: (j, 0)),
                  pl.BlockSpec((tq, GROUP_Q_DIM), lambda i, g, j: (j, 0)),
                  _const_spec((GROUP_Q_DIM, GROUP_Q_DIM)),
                  _const_spec((1, GROUP_Q_DIM))],
        out_specs=pl.BlockSpec((1, tq, GROUP_Q_DIM), lambda i, g, j: (i, j, g)),
        out_shape=jax.ShapeDtypeStruct((b, s, ATTN_DIM), BF16),
        scratch_shapes=[pltpu.VMEM((vq, HEAD_DIM), BF16),
                        pltpu.VMEM((1, vq), F32),
                        pltpu.VMEM((VT_ROWS, vq), F32),
                        pltpu.VMEM((ATTN_PIPE_DEPTH, ATTN_KEY_SUB, ATTN_Q_SLAB), F32)],
        compiler_params=_cparams("parallel", "parallel", "arbitrary"),
        name="attention",
    )(q, k_heads, vt_heads, cos, sin_signed, ones_blk, qw)


def _split3(v):
    hi = v.astype(BF16)
    r1 = v - hi.astype(F32)
    mid = r1.astype(BF16)
    lo = (r1 - mid.astype(F32)).astype(BF16)
    return hi, mid, lo


def _expand_heads(v, e3_ref):
    parts = jnp.concatenate(_split3(v), axis=1)
    return _dot(parts, e3_ref[...])


def _conv_silu(main_ref, prev_ref, next_ref, cw_ref, cb_ref, xs_ref, prev_ok, next_ok, rows):
    base = 8
    xs_ref[base:base + rows, :] = main_ref[0].astype(F32)
    xs_ref[base - 1:base, :] = prev_ref[0, HALO - 1:HALO, :].astype(F32) * prev_ok
    xs_ref[base + rows:base + rows + 1, :] = next_ref[0, 0:1, :].astype(F32) * next_ok
    conv = (xs_ref[base - 1:base - 1 + rows, :] * cw_ref[0:1, :]
            + xs_ref[base:base + rows, :] * cw_ref[1:2, :]
            + xs_ref[base + 1:base + 1 + rows, :] * cw_ref[2:3, :] + cb_ref[...])
    return conv * _sigmoid(conv)


def _ssd_kernel(*refs, reverse, nc):
    if reverse:
        (xbc_ref, prev_ref, next_ref, dt_ref, z_ref, yf_ref, cw_ref, cb_ref, dtb_ref, a_ref,
         tri_ref, e3_ref, dskip_ref, nw_ref, o_ref, xs_ref, h_ref) = refs
    else:
        (xbc_ref, prev_ref, next_ref, dt_ref, cw_ref, cb_ref, dtb_ref, a_ref,
         tri_ref, e3_ref, o_ref, xs_ref, h_ref) = refs
    L = CHUNK
    c = pl.program_id(1)
    cc = nc - 1 - c if reverse else c

    @pl.when(c == 0)
    def _():
        h_ref[...] = jnp.zeros(h_ref.shape, F32)

    prev_ok = (cc > 0).astype(F32)
    next_ok = (cc < nc - 1).astype(F32)
    xc = _conv_silu(xbc_ref, prev_ref, next_ref, cw_ref, cb_ref, xs_ref, prev_ok, next_ok, L)
    x = xc[:, :D_INNER]

    d0 = N_SSM_HEADS if reverse else 0
    dt_raw = dt_ref[0][:, d0:d0 + N_SSM_HEADS] + dtb_ref[...]
    dtv = jnp.maximum(dt_raw, 0.0) + jnp.log(1.0 + jnp.exp(-jnp.abs(dt_raw)))
    da = dtv * a_ref[...]

    tri = tri_ref[...]
    da3 = jnp.concatenate(_split3(da), axis=1)
    cum3 = _dot(tri, da3)
    nh = N_SSM_HEADS
    cum = cum3[:, :nh] + cum3[:, nh:2 * nh] + cum3[:, 2 * nh:3 * nh]
    cum3_t = lax.dot_general(da3, tri, (((0,), (1,)), ((), ())), preferred_element_type=F32)
    cum_t = cum3_t[:nh] + cum3_t[nh:2 * nh] + cum3_t[2 * nh:3 * nh]
    last = 0 if reverse else L - 1
    tot = cum[last:last + 1, :]

    dt_e = _expand_heads(dtv, e3_ref)
    grow_e = _expand_heads(jnp.exp(cum), e3_ref)
    tail_e = _expand_heads(dtv * jnp.exp(tot - cum), e3_ref)
    xdt = (x * dt_e).astype(BF16)
    xtail = (x * tail_e).astype(BF16)
    chunk_decay = grow_e[last:last + 1, :]

    li = lax.broadcasted_iota(jnp.int32, (L, L), 0)
    si = lax.broadcasted_iota(jnp.int32, (L, L), 1)
    mask = (si >= li) if reverse else (si <= li)
    lane = lax.broadcasted_iota(jnp.int32, (L, 2 * SSM_HEADDIM), 1)
    first_head = lane < SSM_HEADDIM

    y_parts = []
    for g in range(N_SSM_GROUPS):
        b_g = xc[:, D_INNER + g * D_STATE:D_INNER + (g + 1) * D_STATE]
        c_g = xc[:, D_INNER + (N_SSM_GROUPS + g) * D_STATE:D_INNER + (N_SSM_GROUPS + g + 1) * D_STATE]
        b_bf = b_g.astype(BF16)
        c_bf = c_g.astype(BF16)
        cb = _dot_nt(c_bf, b_bf)
        gs = slice(g * GROUP_INNER, (g + 1) * GROUP_INNER)
        h_in = h_ref[g]
        y_off = _dot(c_bf, h_in.astype(BF16)) * grow_e[:, gs]
        h_ref[g] = h_in * chunk_decay[:, gs] + _dot(b_g.T.astype(BF16), xtail[:, gs])
        diag = []
        for pr in range(HEADS_PER_GROUP // 2):
            ms = []
            for hh in (g * HEADS_PER_GROUP + 2 * pr, g * HEADS_PER_GROUP + 2 * pr + 1):
                seg = cum[:, hh:hh + 1] - cum_t[hh:hh + 1, :]
                dec = jnp.exp(jnp.where(mask, seg, MASKED_SEG))
                ms.append((cb * dec).astype(BF16))
            lhs = jnp.concatenate(ms, axis=1)
            xp = xdt[:, g * GROUP_INNER + pr * 128:g * GROUP_INNER + (pr + 1) * 128]
            zero = jnp.zeros_like(xp)
            rhs = jnp.concatenate([jnp.where(first_head, xp, zero),
                                   jnp.where(first_head, zero, xp)], axis=0)
            diag.append(_dot(lhs, rhs))
        y_parts.append(jnp.concatenate(diag, axis=1) + y_off)
    y = jnp.concatenate(y_parts, axis=1)

    if not reverse:
        o_ref[0] = y.astype(o_ref.dtype)
    else:
        y = y + yf_ref[0].astype(F32) + x * dskip_ref[...]
        zf = z_ref[0].astype(F32)
        y = y * (zf * _sigmoid(zf))
        outs = []
        for g in range(N_SSM_GROUPS):
            yg = y[:, g * GROUP_INNER:(g + 1) * GROUP_INNER]
            outs.append(yg * lax.rsqrt(jnp.mean(yg * yg, axis=-1, keepdims=True) + NORM_EPS))
        o_ref[0] = (jnp.concatenate(outs, axis=1) * nw_ref[...]).astype(o_ref.dtype)


def _ssd_pass(xbc, dt, consts, reverse, extra=()):
    b, s, _ = xbc.shape
    nc = s // CHUNK
    nhalo = s // HALO
    per = CHUNK // HALO

    def pos(c):
        return nc - 1 - c if reverse else c

    chunk_map = lambda i, c: (i, pos(c), 0)
    prev_map = lambda i, c: (i, jnp.maximum(pos(c) * per - 1, 0), 0)
    next_map = lambda i, c: (i, jnp.minimum((pos(c) + 1) * per, nhalo - 1), 0)
    in_specs = [pl.BlockSpec((1, CHUNK, CONV_DIM), chunk_map),
                pl.BlockSpec((1, HALO, CONV_DIM), prev_map),
                pl.BlockSpec((1, HALO, CONV_DIM), next_map),
                pl.BlockSpec((1, CHUNK, 2 * N_SSM_HEADS), chunk_map)]
    args = [xbc, xbc, xbc, dt]
    for e in extra:
        in_specs.append(pl.BlockSpec((1, CHUNK, D_INNER), chunk_map))
        args.append(e)
    for cst in consts:
        in_specs.append(_const_spec(cst.shape))
        args.append(cst)
    kern = functools.partial(_ssd_kernel, reverse=reverse, nc=nc)
    return pl.pallas_call(
        kern,
        grid=(b, nc),
        in_specs=in_specs,
        out_specs=pl.BlockSpec((1, CHUNK, D_INNER), chunk_map),
        out_shape=jax.ShapeDtypeStruct((b, s, D_INNER), BF16),
        scratch_shapes=[pltpu.VMEM((CHUNK + 16, CONV_DIM), F32),
                        pltpu.VMEM((N_SSM_GROUPS, D_STATE, GROUP_INNER), F32)],
        compiler_params=_cparams("parallel", "arbitrary"),
        name="ssd_bwd" if reverse else "ssd_fwd",
    )(*args)


def _merge_kernel(x_ref, attn_ref, ssd_ref, gate_ref, gb_ref, wa_ref, ws_ref, wo_ref, nw_ref,
                  x1_ref, h2_ref):
    g = _sigmoid(gate_ref[...].astype(F32) + gb_ref[...])
    merged = (g[:, :D_MODEL] * _dot(attn_ref[...], wa_ref[...])
              + g[:, D_MODEL:] * _dot(ssd_ref[...], ws_ref[...]))
    x1 = x_ref[...] + _dot(merged.astype(BF16), wo_ref[...])
    x1_ref[...] = x1
    h2_ref[...] = _rms_rows(x1, nw_ref[...]).astype(BF16)


def _merge(x2d, attn, ssd, gates, gate_b, wa, ws, wo, norm2_w):
    t = x2d.shape[0]
    tm = min(ROW_TILE, t)
    row = lambda w: pl.BlockSpec((tm, w), lambda i: (i, 0))
    return pl.pallas_call(
        _merge_kernel,
        grid=(t // tm,),
        in_specs=[row(D_MODEL), row(ATTN_DIM), row(D_INNER), row(2 * D_MODEL),
                  _const_spec(gate_b.shape), _const_spec(wa.shape), _const_spec(ws.shape),
                  _const_spec(wo.shape), _const_spec(norm2_w.shape)],
        out_specs=[row(D_MODEL), row(D_MODEL)],
        out_shape=[jax.ShapeDtypeStruct((t, D_MODEL), F32),
                   jax.ShapeDtypeStruct((t, D_MODEL), BF16)],
        compiler_params=_cparams("parallel"),
        name="merge",
    )(x2d, attn, ssd, gates, gate_b, wa, ws, wo, norm2_w)


def _ffn_up_kernel(h_ref, w_ref, u_ref):
    h = h_ref[...]
    n = w_ref.shape[1]
    for c in range(0, n, _N_CHUNK):
        u_ref[:, c:c + _N_CHUNK] = _dot(h, w_ref[:, c:c + _N_CHUNK]).astype(u_ref.dtype)


def _ffn_up(h2, w_up):
    t = h2.shape[0]
    tm = min(ROW_TILE, t)
    n = w_up.shape[1]
    return pl.pallas_call(
        _ffn_up_kernel,
        grid=(t // tm,),
        in_specs=[pl.BlockSpec((tm, D_MODEL), lambda i: (i, 0)), _const_spec(w_up.shape)],
        out_specs=pl.BlockSpec((tm, n), lambda i: (i, 0)),
        out_shape=jax.ShapeDtypeStruct((t, n), BF16),
        compiler_params=_cparams("parallel"),
        name="ffn_up",
    )(h2, w_up)


def _gelu_tanh(x):
    return 0.5 * x * (1.0 + jnp.tanh(math.sqrt(2.0 / math.pi) * (x + 0.044715 * (x * x * x))))


def _ffn_down_kernel(u_ref, prev_ref, next_ref, x1_ref, p_ref, cw_ref, cb_ref, wd_ref, wple_ref,
                     wpg_ref, bpg_ref, fw_ref, o_ref, us_ref, *, tm, tiles_per_seq):
    i = pl.program_id(0)
    j = i % tiles_per_seq
    prev_ok = (j > 0).astype(F32)
    next_ok = (j < tiles_per_seq - 1).astype(F32)
    base = 8
    us_ref[base:base + tm, :] = u_ref[...].astype(F32)
    us_ref[base - 1:base, :] = prev_ref[HALO - 1:HALO, :].astype(F32) * prev_ok
    us_ref[base + tm:base + tm + 1, :] = next_ref[0:1, :].astype(F32) * next_ok

    def conv(cols):
        return (us_ref[base - 1:base - 1 + tm, cols] * cw_ref[0:1, cols]
                + us_ref[base:base + tm, cols] * cw_ref[1:2, cols]
                + us_ref[base + 1:base + 1 + tm, cols] * cw_ref[2:3, cols] + cb_ref[:, cols])

    act = (_gelu_tanh(conv(slice(0, D_FF))) * conv(slice(D_FF, 2 * D_FF))).astype(BF16)
    x2 = x1_ref[...] + _dot(act, wd_ref[...])
    pg = _sigmoid(_dot(x2.astype(BF16), wpg_ref[...]) + bpg_ref[...])
    x3 = x2 + pg * _dot(p_ref[...].astype(BF16), wple_ref[...])
    o_ref[...] = _rms_rows(x3, fw_ref[...])


def _ffn_down(u, x1, p2d, seq, conv_w, conv_b, w_down, w_ple, w_pg, b_pg, final_w):
    t = u.shape[0]
    tm = min(FFN_ROW_TILE, seq)
    tiles_per_seq = seq // tm
    per = tm // HALO
    nhalo = t // HALO
    n = u.shape[1]
    kern = functools.partial(_ffn_down_kernel, tm=tm, tiles_per_seq=tiles_per_seq)
    consts = (conv_w, conv_b, w_down, w_ple, w_pg, b_pg, final_w)
    return pl.pallas_call(
        kern,
        grid=(t // tm,),
        in_specs=[pl.BlockSpec((tm, n), lambda i: (i, 0)),
                  pl.BlockSpec((HALO, n), lambda i: (jnp.maximum(i * per - 1, 0), 0)),
                  pl.BlockSpec((HALO, n), lambda i: (jnp.minimum((i + 1) * per, nhalo - 1), 0)),
                  pl.BlockSpec((tm, D_MODEL), lambda i: (i, 0)),
                  pl.BlockSpec((tm, PLE_DIM), lambda i: (i, 0))]
                 + [_const_spec(c.shape) for c in consts],
        out_specs=pl.BlockSpec((tm, D_MODEL), lambda i: (i, 0)),
        out_shape=jax.ShapeDtypeStruct((t, D_MODEL), F32),
        scratch_shapes=[pltpu.VMEM((tm + 16, n), F32)],
        compiler_params=_cparams("parallel"),
        name="ffn_down",
    )(u, u, u, x1, p2d, *consts)


def _rope_tables(n_tokens):
    rows = n_tokens // GRID_W
    row_idx = jnp.repeat(jnp.arange(rows, dtype=F32), GRID_W)
    col_idx = jnp.tile(jnp.arange(GRID_W, dtype=F32), rows)
    inv_freq = ROPE_THETA ** (-jnp.arange(0, AXIS_ROT_DIM, 2, dtype=F32) / AXIS_ROT_DIM)
    ang = jnp.concatenate([row_idx[:, None] * inv_freq, col_idx[:, None] * inv_freq], axis=-1)
    cos, sin = jnp.cos(ang), jnp.sin(ang)
    cos_h = jnp.concatenate([cos, cos], axis=-1)
    sin_h = jnp.concatenate([-sin, sin], axis=-1)
    return jnp.tile(cos_h, (1, N_KV_HEADS)), jnp.tile(sin_h, (1, N_KV_HEADS))


def _prepare(norm1_w, w_in, ssm_conv_w, ssm_conv_b, dt_bias, a_log, d_skip, ssd_norm_w,
             q_norm_w, k_norm_w, gate_b, w_attn_branch, w_ssd_branch, w_out, norm2_w, w_up,
             ffn_conv_w, ffn_conv_b, w_down, w_ple, w_ple_gate, b_ple_gate, final_norm_w):
    perm = jnp.concatenate([jnp.arange(0, HEAD_DIM, 2), jnp.arange(1, HEAD_DIM, 2)])
    w = w_in[0]
    sizes = (ATTN_DIM, KV_DIM, KV_DIM, D_INNER, CONV_DIM, 2 * N_SSM_HEADS, 2 * D_MODEL)
    parts, start = [], 0
    for sz in sizes:
        parts.append(w[:, start:start + sz])
        start += sz
    wq, wk, wv, wz, wxbc, wdt, wg = parts
    wq = wq.reshape(D_MODEL, N_HEADS, HEAD_DIM)[:, :, perm].reshape(D_MODEL, ATTN_DIM)
    wk = wk.reshape(D_MODEL, N_KV_HEADS, HEAD_DIM)[:, :, perm].reshape(D_MODEL, KV_DIM)
    w_all = jnp.concatenate([wq, wk, wv, wz, wxbc, wg, wdt], axis=1).astype(BF16)

    head_id = jnp.arange(KV_DIM) // HEAD_DIM
    ones_blk = (head_id[:, None] == head_id[None, :]).astype(BF16)
    li = jnp.arange(CHUNK)
    tri_fwd = (li[None, :] <= li[:, None]).astype(BF16)
    tri_bwd = (li[None, :] >= li[:, None]).astype(BF16)
    chan_head = jnp.arange(D_INNER) // SSM_HEADDIM
    e1 = (jnp.arange(N_SSM_HEADS)[:, None] == chan_head[None, :]).astype(BF16)
    a = -jnp.exp(a_log[0].astype(F32))
    return dict(
        norm1_w=norm1_w[0][None, :], w_all=w_all,
        ones_blk=ones_blk,
        qw=jnp.tile(q_norm_w[0][perm], GQA_GROUP)[None, :],
        kw=jnp.tile(k_norm_w[0][perm], N_KV_HEADS)[None, :],
        conv_w=ssm_conv_w[0], conv_b=ssm_conv_b[0][None, :],
        dtb_f=dt_bias[0][0][None, :], dtb_b=dt_bias[0][1][None, :],
        a_f=a[0][None, :], a_b=a[1][None, :],
        tri_f=tri_fwd, tri_b=tri_bwd, e3=jnp.tile(e1, (3, 1)),
        dskip=jnp.repeat(d_skip[0], SSM_HEADDIM)[None, :], ssd_nw=ssd_norm_w[0][None, :],
        gate_b=gate_b[0][None, :], wa=w_attn_branch[0].astype(BF16),
        ws=w_ssd_branch[0].astype(BF16), wo=w_out[0].astype(BF16), norm2_w=norm2_w[0][None, :],
        w_up=w_up[0].astype(BF16), ffn_cw=ffn_conv_w[0], ffn_cb=ffn_conv_b[0][None, :],
        w_down=w_down[0].astype(BF16), w_ple=w_ple[0].astype(BF16),
        w_pg=w_ple_gate[0].astype(BF16), b_pg=b_ple_gate[0][None, :],
        final_w=final_norm_w[None, :])


def _trunk(x, p, prm):
    b, s, _ = x.shape
    t = b * s
    x2d = x.reshape(t, D_MODEL)
    q, k, v, z, xbc, gates, dt = _in_proj(x2d, prm["norm1_w"], prm["w_all"])
    cos, sin_signed = _rope_tables(s)

    ts = min(KV_PREP_TILE, s)
    k_heads, vt_heads = _kv_prep(k.reshape(b, s, KV_DIM), v.reshape(b, s, KV_DIM), cos, sin_signed,
                                 prm["ones_blk"], prm["kw"], ts)
    attn = _attention(q.reshape(b, s, ATTN_DIM), k_heads, vt_heads, cos, sin_signed,
                      prm["ones_blk"], prm["qw"], min(ATTN_Q_TILE, s))

    xbc3 = xbc.reshape(b, s, CONV_DIM)
    dt3 = dt.reshape(b, s, 2 * N_SSM_HEADS)
    y_fwd = _ssd_pass(xbc3, dt3, (prm["conv_w"], prm["conv_b"], prm["dtb_f"], prm["a_f"],
                                  prm["tri_f"], prm["e3"]), reverse=False)
    ssd = _ssd_pass(xbc3, dt3, (prm["conv_w"], prm["conv_b"], prm["dtb_b"], prm["a_b"],
                                prm["tri_b"], prm["e3"], prm["dskip"], prm["ssd_nw"]),
                    reverse=True, extra=(z.reshape(b, s, D_INNER), y_fwd))

    x1, h2 = _merge(x2d, attn.reshape(t, ATTN_DIM), ssd.reshape(t, D_INNER), gates,
                    prm["gate_b"], prm["wa"], prm["ws"], prm["wo"], prm["norm2_w"])
    u = _ffn_up(h2, prm["w_up"])
    out = _ffn_down(u, x1, p[0].reshape(t, PLE_DIM), s, prm["ffn_cw"], prm["ffn_cb"],
                    prm["w_down"], prm["w_ple"], prm["w_pg"], prm["b_pg"], prm["final_w"])
    return out.reshape(b, s, D_MODEL)


def kernel(x_prompt, x_sample, p_prompt, p_sample, norm1_w, w_in, ssm_conv_w, ssm_conv_b, dt_bias,
           a_log, d_skip, ssd_norm_w, q_norm_w, k_norm_w, gate_b, w_attn_branch, w_ssd_branch,
           w_out, norm2_w, w_up, ffn_conv_w, ffn_conv_b, w_down, w_ple, w_ple_gate, b_ple_gate,
           final_norm_w):
    prm = _prepare(norm1_w, w_in, ssm_conv_w, ssm_conv_b, dt_bias, a_log, d_skip, ssd_norm_w,
                   q_norm_w, k_norm_w, gate_b, w_attn_branch, w_ssd_branch, w_out, norm2_w, w_up,
                   ffn_conv_w, ffn_conv_b, w_down, w_ple, w_ple_gate, b_ple_gate, final_norm_w)
    return (_trunk(x_prompt, p_prompt, prm), _trunk(x_sample, p_sample, prm))
```

```python
import functools
import math

import jax
import jax.numpy as jnp
from jax import lax
from jax.experimental import pallas as pl
from jax.experimental.pallas import tpu as pltpu

F32 = jnp.float32
BF16 = jnp.bfloat16

D_MODEL = 1024
N_HEADS = 16
N_KV_HEADS = 4
HEAD_DIM = 64
GQA_GROUP = N_HEADS // N_KV_HEADS
ATTN_DIM = N_HEADS * HEAD_DIM
KV_DIM = N_KV_HEADS * HEAD_DIM
GROUP_Q_DIM = GQA_GROUP * HEAD_DIM
VT_ROWS = HEAD_DIM + 16
AXIS_ROT_DIM = HEAD_DIM // 2
ROPE_THETA = 10000.0
GRID_W = 64

D_INNER = 2048
SSM_HEADDIM = 64
N_SSM_HEADS = D_INNER // SSM_HEADDIM
N_SSM_GROUPS = 4
HEADS_PER_GROUP = N_SSM_HEADS // N_SSM_GROUPS
D_STATE = 128
GROUP_INNER = HEADS_PER_GROUP * SSM_HEADDIM
CONV_DIM = D_INNER + 2 * N_SSM_GROUPS * D_STATE
CHUNK = 128

D_FF = 2816
PLE_DIM = 256
NORM_EPS = 1e-6

V7X_LANES = 128
V7X_BF16_SUBLANES = 16
V7X_VMEM_LIMIT_BYTES = 60000 * 1024

MASKED_SEG = -1e30
LOG2E = 1.4426950408889634

ROW_TILE = 256
KV_PREP_TILE = 512
ATTN_Q_TILE = 512
ATTN_KEY_SUB = 256
ATTN_Q_SLAB = 512
ATTN_PIPE_DEPTH = 3
FFN_ROW_TILE = 256
HALO = V7X_BF16_SUBLANES


def _cparams(*sem):
    return pltpu.CompilerParams(dimension_semantics=sem, vmem_limit_bytes=V7X_VMEM_LIMIT_BYTES)


def _const_spec(shape):
    nd = len(shape)
    return pl.BlockSpec(shape, lambda *_: (0,) * nd)


def _dot(a, b):
    return jnp.dot(a, b, preferred_element_type=F32)


def _dot_nt(a, b):
    return lax.dot_general(a, b, (((1,), (1,)), ((), ())), preferred_element_type=F32)


def _rms_rows(x, w):
    return x * lax.rsqrt(jnp.mean(x * x, axis=-1, keepdims=True) + NORM_EPS) * w


def _sigmoid(x):
    return 1.0 / (1.0 + jnp.exp(-x))


_IN_PROJ_OUTS = (
    (ATTN_DIM, BF16), (KV_DIM, BF16), (KV_DIM, BF16), (D_INNER, BF16), (CONV_DIM, BF16),
    (2 * D_MODEL, BF16), (2 * V7X_LANES, F32))
_N_CHUNK = 512


def _in_proj_kernel(x_ref, nw_ref, w_ref, *out_refs):
    h = _rms_rows(x_ref[...], nw_ref[...]).astype(BF16)
    col = 0
    for o_ref, (width, _) in zip(out_refs, _IN_PROJ_OUTS):
        for c in range(0, width, _N_CHUNK):
            cw = min(_N_CHUNK, width - c)
            o_ref[:, c:c + cw] = _dot(h, w_ref[:, col + c:col + c + cw]).astype(o_ref.dtype)
        col += width


def _in_proj(x2d, norm_w, w_all):
    t = x2d.shape[0]
    tm = min(ROW_TILE, t)
    n_all = w_all.shape[1]
    return pl.pallas_call(
        _in_proj_kernel,
        grid=(t // tm,),
        in_specs=[pl.BlockSpec((tm, D_MODEL), lambda i: (i, 0)),
                  _const_spec((1, D_MODEL)),
                  _const_spec((D_MODEL, n_all))],
        out_specs=[pl.BlockSpec((tm, w), lambda i: (i, 0)) for w, _ in _IN_PROJ_OUTS],
        out_shape=[jax.ShapeDtypeStruct((t, w), dt) for w, dt in _IN_PROJ_OUTS],
        compiler_params=_cparams("parallel"),
        name="in_proj",
    )(x2d, norm_w, w_all)


def _head_norm_rope(x, ones_blk, w, cos, sin_signed):
    width = x.shape[-1]
    ss = _dot((x * x).astype(BF16), ones_blk)
    xn = x * lax.rsqrt(ss * (1.0 / HEAD_DIM) + NORM_EPS) * w
    lane = lax.broadcasted_iota(jnp.int32, xn.shape, 1)
    half = HEAD_DIM // 2
    partner = jnp.where((lane % HEAD_DIM) < half,
                        pltpu.roll(xn, width - half, 1), pltpu.roll(xn, half, 1))
    return xn * cos + partner * sin_signed


def _kv_prep_kernel(k_ref, v_ref, cos_ref, sin_ref, ones_ref, kw_ref, ko_ref, vto_ref):
    k = _head_norm_rope(k_ref[0].astype(F32), ones_ref[...], kw_ref[...],
                        cos_ref[...], sin_ref[...])
    vt = v_ref[0].astype(F32).T
    ts = vt.shape[1]
    sub = lax.broadcasted_iota(jnp.int32, (VT_ROWS - HEAD_DIM, ts), 0)
    tail = jnp.where(sub == 0, 1.0, 0.0).astype(BF16)
    for h in range(N_KV_HEADS):
        ko_ref[0, h] = k[:, h * HEAD_DIM:(h + 1) * HEAD_DIM].astype(BF16)
        vto_ref[0, h, 0, 0:HEAD_DIM, :] = vt[h * HEAD_DIM:(h + 1) * HEAD_DIM, :].astype(BF16)
        vto_ref[0, h, 0, HEAD_DIM:VT_ROWS, :] = tail


def _kv_prep(k, v, cos, sin_signed, ones_blk, kw, ts):
    b, s, _ = k.shape
    nk = s // ts
    return pl.pallas_call(
        _kv_prep_kernel,
        grid=(b, nk),
        in_specs=[pl.BlockSpec((1, ts, KV_DIM), lambda i, j: (i, j, 0)),
                  pl.BlockSpec((1, ts, KV_DIM), lambda i, j: (i, j, 0)),
                  pl.BlockSpec((ts, KV_DIM), lambda i, j: (j, 0)),
                  pl.BlockSpec((ts, KV_DIM), lambda i, j: (j, 0)),
                  _const_spec((KV_DIM, KV_DIM)),
                  _const_spec((1, KV_DIM))],
        out_specs=[pl.BlockSpec((1, N_KV_HEADS, ts, HEAD_DIM), lambda i, j: (i, 0, j, 0)),
                   pl.BlockSpec((1, N_KV_HEADS, 1, VT_ROWS, ts), lambda i, j: (i, 0, j, 0, 0))],
        out_shape=[jax.ShapeDtypeStruct((b, N_KV_HEADS, s, HEAD_DIM), BF16),
                   jax.ShapeDtypeStruct((b, N_KV_HEADS, nk, VT_ROWS, ts), BF16)],
        compiler_params=_cparams("parallel", "parallel"),
        name="kv_prep",
    )(k, v, cos, sin_signed, ones_blk, kw)


def _attn_kernel(q_ref, k_ref, vt_ref, cos_ref, sin_ref, ones_ref, qw_ref, o_ref,
                 qs_ref, m_ref, acc_ref, sp_ref, *, tq, tk, nk):
    q = _head_norm_rope(q_ref[0].astype(F32), ones_ref[...], qw_ref[...],
                        cos_ref[...], sin_ref[...])
    q_t = (q * (HEAD_DIM ** -0.5 * LOG2E)).T
    for h in range(GQA_GROUP):
        qs_ref[:, h * tq:(h + 1) * tq] = q_t[h * HEAD_DIM:(h + 1) * HEAD_DIM, :].astype(BF16)
    m_ref[...] = jnp.full(m_ref.shape, -jnp.inf, F32)
    acc_ref[...] = jnp.zeros(acc_ref.shape, F32)

    units = [(i, n) for i in range(tk // ATTN_KEY_SUB)
             for n in range(GQA_GROUP * tq // ATTN_Q_SLAB)]
    n_units = len(units)
    depth = min(ATTN_PIPE_DEPTH, n_units)

    def scores(j, unit):
        i, n = unit
        row0 = pl.multiple_of(j * tk + i * ATTN_KEY_SUB, ATTN_KEY_SUB)
        k = k_ref[0, 0, pl.ds(row0, ATTN_KEY_SUB), :]
        return _dot(k, qs_ref[:, n * ATTN_Q_SLAB:(n + 1) * ATTN_Q_SLAB])

    def softmax_pv(j, unit, s_t):
        i, n = unit
        cs = slice(n * ATTN_Q_SLAB, (n + 1) * ATTN_Q_SLAB)
        v_i = vt_ref[0, 0, j, :, i * ATTN_KEY_SUB:(i + 1) * ATTN_KEY_SUB]
        m_prev = m_ref[:, cs]
        s_b = s_t.astype(BF16)
        m_new = jnp.maximum(m_prev, jnp.max(s_b, axis=0, keepdims=True).astype(F32))
        p = jnp.exp2(s_b - m_new.astype(BF16))
        alpha = jnp.exp2(m_prev - m_new)
        acc_ref[:, cs] = alpha * acc_ref[:, cs] + _dot(v_i, p)
        m_ref[:, cs] = m_new

    for d in range(depth):
        sp_ref[d] = scores(0, units[d])

    def kv_step(j, carry):
        j_next = jnp.minimum(j + 1, nk - 1)
        pending = {}
        for idx, u in enumerate(units):
            ahead = idx + depth
            if ahead < n_units:
                pending[ahead] = scores(j, units[ahead])
            s_t = sp_ref[idx] if idx < depth else pending.pop(idx)
            softmax_pv(j, u, s_t)
            if ahead >= n_units:
                sp_ref[ahead - n_units] = scores(j_next, units[ahead - n_units])
        return carry

    lax.fori_loop(0, nk, kv_step, 0)
    out_t = acc_ref[0:HEAD_DIM, :] * (1.0 / acc_ref[HEAD_DIM:HEAD_DIM + 1, :])
    out_t = jnp.concatenate([out_t[:, h * tq:(h + 1) * tq] for h in range(GQA_GROUP)], axis=0)
    o_ref[0] = out_t.T.astype(o_ref.dtype)


def _attention(q, k_heads, vt_heads, cos, sin_signed, ones_blk, qw, tq):
    b, s, _ = q.shape
    nk, tk = vt_heads.shape[2], vt_heads.shape[4]
    vq = GQA_GROUP * tq
    kern = functools.partial(_attn_kernel, tq=tq, tk=tk, nk=nk)
    return pl.pallas_call(
        kern,
        grid=(b, N_KV_HEADS, s // tq),
        in_specs=[pl.BlockSpec((1, tq, GROUP_Q_DIM), lambda i, g, j: (i, j, g)),
                  pl.BlockSpec((1, 1, s, HEAD_DIM), lambda i, g, j: (i, g, 0, 0)),
                  pl.BlockSpec((1, 1, nk, VT_ROWS, tk), lambda i, g, j: (i, g, 0, 0, 0)),
                  pl.BlockSpec((tq, GROUP_Q_DIM), lambda i, g, j: (j, 0)),
                  pl.BlockSpec((tq, GROUP_Q_DIM), lambda i, g, j: (j, 0)),
                  _const_spec((GROUP_Q_DIM, GROUP_Q_DIM)),
                  _const_spec((1, GROUP_Q_DIM))],
        out_specs=pl.BlockSpec((1, tq, GROUP_Q_DIM), lambda i, g, j: (i, j, g)),
        out_shape=jax.ShapeDtypeStruct((b, s, ATTN_DIM), BF16),
        scratch_shapes=[pltpu.VMEM((HEAD_DIM, vq), BF16),
                        pltpu.VMEM((1, vq), F32),
                        pltpu.VMEM((VT_ROWS, vq), F32),
                        pltpu.VMEM((ATTN_PIPE_DEPTH, ATTN_KEY_SUB, ATTN_Q_SLAB), F32)],
        compiler_params=_cparams("parallel", "parallel", "arbitrary"),
        name="attention",
    )(q, k_heads, vt_heads, cos, sin_signed, ones_blk, qw)


HEAD_LANE_COPIES = V7X_LANES // N_SSM_HEADS


def _split_select(v):
    hi = v.astype(BF16)
    r1 = v - hi.astype(F32)
    mid = r1.astype(BF16)
    lo = (r1 - mid.astype(F32)).astype(BF16)
    copy = lax.broadcasted_iota(jnp.int32, v.shape, 1) // N_SSM_HEADS
    return jnp.where(copy == 0, hi, jnp.where(copy == 1, mid, jnp.where(copy == 2, lo, jnp.zeros_like(lo))))


def _expand_heads(v, e_ref):
    return _dot(_split_select(v), e_ref[...])


def _silu(x):
    return x * (1.0 / (1.0 + jnp.exp2(x * (-LOG2E))))


def _xbc_conv_kernel(main_ref, prev_ref, next_ref, sp_ref, sn_ref, cw_ref, cb_ref, o_ref, *, nc):
    cc = pl.program_id(1)
    prev_ok = (cc > 0).astype(BF16)
    next_ok = (cc < nc - 1).astype(BF16)
    for c0 in range(0, CONV_DIM, _N_CHUNK):
        cols = slice(c0, c0 + _N_CHUNK)
        main = main_ref[0, :, cols]
        ext = jnp.concatenate([prev_ref[0, :, cols] * prev_ok, main, next_ref[0, :, cols] * next_ok],
                              axis=0)
        conv = (_dot(sp_ref[...], ext) * cw_ref[0:1, cols] + main.astype(F32) * cw_ref[1:2, cols]
                + _dot(sn_ref[...], ext) * cw_ref[2:3, cols] + cb_ref[:, cols])
        o_ref[0, :, cols] = _silu(conv).astype(o_ref.dtype)


def _xbc_conv(xbc, shift_prev, shift_next, conv_w, conv_b):
    b, s, _ = xbc.shape
    nc = s // CHUNK
    nhalo = s // HALO
    per = CHUNK // HALO
    consts = (shift_prev, shift_next, conv_w, conv_b)
    return pl.pallas_call(
        functools.partial(_xbc_conv_kernel, nc=nc),
        grid=(b, nc),
        in_specs=[pl.BlockSpec((1, CHUNK, CONV_DIM), lambda i, c: (i, c, 0)),
                  pl.BlockSpec((1, HALO, CONV_DIM), lambda i, c: (i, jnp.maximum(c * per - 1, 0), 0)),
                  pl.BlockSpec((1, HALO, CONV_DIM),
                               lambda i, c: (i, jnp.minimum((c + 1) * per, nhalo - 1), 0))]
                 + [_const_spec(c.shape) for c in consts],
        out_specs=pl.BlockSpec((1, CHUNK, CONV_DIM), lambda i, c: (i, c, 0)),
        out_shape=jax.ShapeDtypeStruct((b, s, CONV_DIM), BF16),
        compiler_params=_cparams("parallel", "parallel"),
        name="xbc_conv",
    )(xbc, xbc, xbc, *consts)


def _ssd_kernel(*refs, reverse):
    if reverse:
        (xc_ref, dt_ref, z_ref, yf_ref, dtb_ref, a_ref, tri_ref, e_ref, dskip_ref, nw_ref,
         o_ref, h_ref) = refs
    else:
        xc_ref, dt_ref, dtb_ref, a_ref, tri_ref, e_ref, o_ref, h_ref = refs
    L = CHUNK
    nh = N_SSM_HEADS

    @pl.when(pl.program_id(1) == 0)
    def _():
        h_ref[...] = jnp.zeros(h_ref.shape, F32)

    x = xc_ref[0, :, 0:D_INNER].astype(F32)

    dt_raw = dt_ref[0] + dtb_ref[...]
    dtv = jnp.maximum(dt_raw, 0.0) + jnp.log(1.0 + jnp.exp(-jnp.abs(dt_raw)))
    da = dtv * a_ref[...]

    tri = tri_ref[...]
    da_sel = _split_select(da)
    part = _dot(tri, da_sel)
    cum = part
    for r in range(1, HEAD_LANE_COPIES):
        cum = cum + pltpu.roll(part, r * nh, 1)
    part_t = lax.dot_general(da_sel, tri, (((0,), (1,)), ((), ())), preferred_element_type=F32)
    cum_t = part_t[:nh] + part_t[nh:2 * nh] + part_t[2 * nh:3 * nh]
    last = 0 if reverse else L - 1
    tot = cum[last:last + 1, :]

    dt_e = _expand_heads(dtv, e_ref)
    grow_e = _expand_heads(jnp.exp(cum), e_ref)
    tail_e = _expand_heads(dtv * jnp.exp(tot - cum), e_ref)
    xdt = (x * dt_e).astype(BF16)
    xtail = (x * tail_e).astype(BF16)
    chunk_decay = grow_e[last:last + 1, :]

    li = lax.broadcasted_iota(jnp.int32, (L, L), 0)
    si = lax.broadcasted_iota(jnp.int32, (L, L), 1)
    mask = (si >= li) if reverse else (si <= li)
    lane = lax.broadcasted_iota(jnp.int32, (L, 2 * SSM_HEADDIM), 1)
    first_head = lane < SSM_HEADDIM

    y_parts = []
    for g in range(N_SSM_GROUPS):
        b0 = D_INNER + g * D_STATE
        c0 = D_INNER + (N_SSM_GROUPS + g) * D_STATE
        b_bf = xc_ref[0, :, b0:b0 + D_STATE]
        c_bf = xc_ref[0, :, c0:c0 + D_STATE]
        cb = _dot_nt(c_bf, b_bf)
        gs = slice(g * GROUP_INNER, (g + 1) * GROUP_INNER)
        h_in = h_ref[g]
        y_off = _dot(c_bf, h_in.astype(BF16)) * grow_e[:, gs]
        b_t = b_bf.astype(F32).T.astype(BF16)
        h_ref[g] = h_in * chunk_decay[:, gs] + _dot(b_t, xtail[:, gs])
        diag = []
        for pr in range(HEADS_PER_GROUP // 2):
            ms = []
            for hh in (g * HEADS_PER_GROUP + 2 * pr, g * HEADS_PER_GROUP + 2 * pr + 1):
                seg = cum[:, hh:hh + 1] - cum_t[hh:hh + 1, :]
                dec = jnp.exp(jnp.where(mask, seg, MASKED_SEG))
                ms.append((cb * dec).astype(BF16))
            lhs = jnp.concatenate(ms, axis=1)
            xp = xdt[:, g * GROUP_INNER + pr * 128:g * GROUP_INNER + (pr + 1) * 128]
            zero = jnp.zeros_like(xp)
            rhs = jnp.concatenate([jnp.where(first_head, xp, zero),
                                   jnp.where(first_head, zero, xp)], axis=0)
            diag.append(_dot(lhs, rhs))
        y_parts.append(jnp.concatenate(diag, axis=1) + y_off)
    y = jnp.concatenate(y_parts, axis=1)

    if not reverse:
        o_ref[0] = y.astype(o_ref.dtype)
    else:
        y = y + yf_ref[0].astype(F32) + x * dskip_ref[...]
        y = y * _silu(z_ref[0].astype(F32))
        outs = []
        for g in range(N_SSM_GROUPS):
            yg = y[:, g * GROUP_INNER:(g + 1) * GROUP_INNER]
            outs.append(yg * lax.rsqrt(jnp.mean(yg * yg, axis=-1, keepdims=True) + NORM_EPS))
        o_ref[0] = (jnp.concatenate(outs, axis=1) * nw_ref[...]).astype(o_ref.dtype)


def _ssd_pass(xc, dt, consts, reverse, extra=()):
    b, s, _ = xc.shape
    nc = s // CHUNK
    chunk_map = (lambda i, c: (i, nc - 1 - c, 0)) if reverse else (lambda i, c: (i, c, 0))
    dt_map = (lambda i, c: (i, nc - 1 - c, 1)) if reverse else (lambda i, c: (i, c, 0))
    in_specs = [pl.BlockSpec((1, CHUNK, CONV_DIM), chunk_map),
                pl.BlockSpec((1, CHUNK, V7X_LANES), dt_map)]
    args = [xc, dt]
    for e in extra:
        in_specs.append(pl.BlockSpec((1, CHUNK, D_INNER), chunk_map))
        args.append(e)
    for cst in consts:
        in_specs.append(_const_spec(cst.shape))
        args.append(cst)
    return pl.pallas_call(
        functools.partial(_ssd_kernel, reverse=reverse),
        grid=(b, nc),
        in_specs=in_specs,
        out_specs=pl.BlockSpec((1, CHUNK, D_INNER), chunk_map),
        out_shape=jax.ShapeDtypeStruct((b, s, D_INNER), BF16),
        scratch_shapes=[pltpu.VMEM((N_SSM_GROUPS, D_STATE, GROUP_INNER), F32)],
        compiler_params=_cparams("parallel", "arbitrary"),
        name="ssd_bwd" if reverse else "ssd_fwd",
    )(*args)


def _merge_kernel(x_ref, attn_ref, ssd_ref, gate_ref, gb_ref, wa_ref, ws_ref, wo_ref, nw_ref,
                  x1_ref, h2_ref):
    g = _sigmoid(gate_ref[...].astype(F32) + gb_ref[...])
    merged = (g[:, :D_MODEL] * _dot(attn_ref[...], wa_ref[...])
              + g[:, D_MODEL:] * _dot(ssd_ref[...], ws_ref[...]))
    x1 = x_ref[...] + _dot(merged.astype(BF16), wo_ref[...])
    x1_ref[...] = x1
    h2_ref[...] = _rms_rows(x1, nw_ref[...]).astype(BF16)


def _merge(x2d, attn, ssd, gates, gate_b, wa, ws, wo, norm2_w):
    t = x2d.shape[0]
    tm = min(ROW_TILE, t)
    row = lambda w: pl.BlockSpec((tm, w), lambda i: (i, 0))
    return pl.pallas_call(
        _merge_kernel,
        grid=(t // tm,),
        in_specs=[row(D_MODEL), row(ATTN_DIM), row(D_INNER), row(2 * D_MODEL),
                  _const_spec(gate_b.shape), _const_spec(wa.shape), _const_spec(ws.shape),
                  _const_spec(wo.shape), _const_spec(norm2_w.shape)],
        out_specs=[row(D_MODEL), row(D_MODEL)],
        out_shape=[jax.ShapeDtypeStruct((t, D_MODEL), F32),
                   jax.ShapeDtypeStruct((t, D_MODEL), BF16)],
        compiler_params=_cparams("parallel"),
        name="merge",
    )(x2d, attn, ssd, gates, gate_b, wa, ws, wo, norm2_w)


def _ffn_up_kernel(h_ref, w_ref, u_ref):
    h = h_ref[...]
    n = w_ref.shape[1]
    for c in range(0, n, _N_CHUNK):
        u_ref[:, c:c + _N_CHUNK] = _dot(h, w_ref[:, c:c + _N_CHUNK]).astype(u_ref.dtype)


def _ffn_up(h2, w_up):
    t = h2.shape[0]
    tm = min(ROW_TILE, t)
    n = w_up.shape[1]
    return pl.pallas_call(
        _ffn_up_kernel,
        grid=(t // tm,),
        in_specs=[pl.BlockSpec((tm, D_MODEL), lambda i: (i, 0)), _const_spec(w_up.shape)],
        out_specs=pl.BlockSpec((tm, n), lambda i: (i, 0)),
        out_shape=jax.ShapeDtypeStruct((t, n), BF16),
        compiler_params=_cparams("parallel"),
        name="ffn_up",
    )(h2, w_up)


def _gelu_tanh(x):
    return 0.5 * x * (1.0 + jnp.tanh(math.sqrt(2.0 / math.pi) * (x + 0.044715 * (x * x * x))))


def _ffn_down_kernel(u_ref, prev_ref, next_ref, x1_ref, p_ref, cw_ref, cb_ref, wd_ref, wple_ref,
                     wpg_ref, bpg_ref, fw_ref, o_ref, us_ref, *, tm, tiles_per_seq):
    i = pl.program_id(0)
    j = i % tiles_per_seq
    prev_ok = (j > 0).astype(F32)
    next_ok = (j < tiles_per_seq - 1).astype(F32)
    base = 8
    us_ref[base:base + tm, :] = u_ref[...].astype(F32)
    us_ref[base - 1:base, :] = prev_ref[HALO - 1:HALO, :].astype(F32) * prev_ok
    us_ref[base + tm:base + tm + 1, :] = next_ref[0:1, :].astype(F32) * next_ok

    def conv(cols):
        return (us_ref[base - 1:base - 1 + tm, cols] * cw_ref[0:1, cols]
                + us_ref[base:base + tm, cols] * cw_ref[1:2, cols]
                + us_ref[base + 1:base + 1 + tm, cols] * cw_ref[2:3, cols] + cb_ref[:, cols])

    act = (_gelu_tanh(conv(slice(0, D_FF))) * conv(slice(D_FF, 2 * D_FF))).astype(BF16)
    x2 = x1_ref[...] + _dot(act, wd_ref[...])
    pg = _sigmoid(_dot(x2.astype(BF16), wpg_ref[...]) + bpg_ref[...])
    x3 = x2 + pg * _dot(p_ref[...].astype(BF16), wple_ref[...])
    o_ref[...] = _rms_rows(x3, fw_ref[...])


def _ffn_down(u, x1, p2d, seq, conv_w, conv_b, w_down, w_ple, w_pg, b_pg, final_w):
    t = u.shape[0]
    tm = min(FFN_ROW_TILE, seq)
    tiles_per_seq = seq // tm
    per = tm // HALO
    nhalo = t // HALO
    n = u.shape[1]
    kern = functools.partial(_ffn_down_kernel, tm=tm, tiles_per_seq=tiles_per_seq)
    consts = (conv_w, conv_b, w_down, w_ple, w_pg, b_pg, final_w)
    return pl.pallas_call(
        kern,
        grid=(t // tm,),
        in_specs=[pl.BlockSpec((tm, n), lambda i: (i, 0)),
                  pl.BlockSpec((HALO, n), lambda i: (jnp.maximum(i * per - 1, 0), 0)),
                  pl.BlockSpec((HALO, n), lambda i: (jnp.minimum((i + 1) * per, nhalo - 1), 0)),
                  pl.BlockSpec((tm, D_MODEL), lambda i: (i, 0)),
                  pl.BlockSpec((tm, PLE_DIM), lambda i: (i, 0))]
                 + [_const_spec(c.shape) for c in consts],
        out_specs=pl.BlockSpec((tm, D_MODEL), lambda i: (i, 0)),
        out_shape=jax.ShapeDtypeStruct((t, D_MODEL), F32),
        scratch_shapes=[pltpu.VMEM((tm + 16, n), F32)],
        compiler_params=_cparams("parallel"),
        name="ffn_down",
    )(u, u, u, x1, p2d, *consts)


def _rope_tables(n_tokens):
    rows = n_tokens // GRID_W
    row_idx = jnp.repeat(jnp.arange(rows, dtype=F32), GRID_W)
    col_idx = jnp.tile(jnp.arange(GRID_W, dtype=F32), rows)
    inv_freq = ROPE_THETA ** (-jnp.arange(0, AXIS_ROT_DIM, 2, dtype=F32) / AXIS_ROT_DIM)
    ang = jnp.concatenate([row_idx[:, None] * inv_freq, col_idx[:, None] * inv_freq], axis=-1)
    cos, sin = jnp.cos(ang), jnp.sin(ang)
    cos_h = jnp.concatenate([cos, cos], axis=-1)
    sin_h = jnp.concatenate([-sin, sin], axis=-1)
    return jnp.tile(cos_h, (1, N_KV_HEADS)), jnp.tile(sin_h, (1, N_KV_HEADS))


def _prepare(norm1_w, w_in, ssm_conv_w, ssm_conv_b, dt_bias, a_log, d_skip, ssd_norm_w,
             q_norm_w, k_norm_w, gate_b, w_attn_branch, w_ssd_branch, w_out, norm2_w, w_up,
             ffn_conv_w, ffn_conv_b, w_down, w_ple, w_ple_gate, b_ple_gate, final_norm_w):
    perm = jnp.concatenate([jnp.arange(0, HEAD_DIM, 2), jnp.arange(1, HEAD_DIM, 2)])
    w = w_in[0]
    sizes = (ATTN_DIM, KV_DIM, KV_DIM, D_INNER, CONV_DIM, 2 * N_SSM_HEADS, 2 * D_MODEL)
    parts, start = [], 0
    for sz in sizes:
        parts.append(w[:, start:start + sz])
        start += sz
    wq, wk, wv, wz, wxbc, wdt, wg = parts
    wq = wq.reshape(D_MODEL, N_HEADS, HEAD_DIM)[:, :, perm].reshape(D_MODEL, ATTN_DIM)
    wk = wk.reshape(D_MODEL, N_KV_HEADS, HEAD_DIM)[:, :, perm].reshape(D_MODEL, KV_DIM)
    wdt = jnp.concatenate([wdt[:, :N_SSM_HEADS]] * HEAD_LANE_COPIES
                          + [wdt[:, N_SSM_HEADS:]] * HEAD_LANE_COPIES, axis=1)
    w_all = jnp.concatenate([wq, wk, wv, wz, wxbc, wg, wdt], axis=1).astype(BF16)

    head_id = jnp.arange(KV_DIM) // HEAD_DIM
    ones_blk = (head_id[:, None] == head_id[None, :]).astype(BF16)
    li = jnp.arange(CHUNK)
    tri_fwd = (li[None, :] <= li[:, None]).astype(BF16)
    tri_bwd = (li[None, :] >= li[:, None]).astype(BF16)
    chan_head = jnp.arange(D_INNER) // SSM_HEADDIM
    e1 = (jnp.arange(N_SSM_HEADS)[:, None] == chan_head[None, :]).astype(BF16)
    e_rows = jnp.concatenate([e1] * (HEAD_LANE_COPIES - 1) + [jnp.zeros_like(e1)], axis=0)
    ext = jnp.arange(CHUNK + 2 * HALO)
    shift_prev = (ext[None, :] == li[:, None] + HALO - 1).astype(BF16)
    shift_next = (ext[None, :] == li[:, None] + HALO + 1).astype(BF16)
    a = jnp.tile(-jnp.exp(a_log[0].astype(F32)), (1, HEAD_LANE_COPIES))
    dtb = jnp.tile(dt_bias[0], (1, HEAD_LANE_COPIES))
    return dict(
        norm1_w=norm1_w[0][None, :], w_all=w_all,
        ones_blk=ones_blk,
        qw=jnp.tile(q_norm_w[0][perm], GQA_GROUP)[None, :],
        kw=jnp.tile(k_norm_w[0][perm], N_KV_HEADS)[None, :],
        conv_w=ssm_conv_w[0], conv_b=ssm_conv_b[0][None, :],
        dtb_f=dtb[0][None, :], dtb_b=dtb[1][None, :],
        a_f=a[0][None, :], a_b=a[1][None, :],
        tri_f=tri_fwd, tri_b=tri_bwd, e_rows=e_rows,
        shift_prev=shift_prev, shift_next=shift_next,
        dskip=jnp.repeat(d_skip[0], SSM_HEADDIM)[None, :], ssd_nw=ssd_norm_w[0][None, :],
        gate_b=gate_b[0][None, :], wa=w_attn_branch[0].astype(BF16),
        ws=w_ssd_branch[0].astype(BF16), wo=w_out[0].astype(BF16), norm2_w=norm2_w[0][None, :],
        w_up=w_up[0].astype(BF16), ffn_cw=ffn_conv_w[0], ffn_cb=ffn_conv_b[0][None, :],
        w_down=w_down[0].astype(BF16), w_ple=w_ple[0].astype(BF16),
        w_pg=w_ple_gate[0].astype(BF16), b_pg=b_ple_gate[0][None, :],
        final_w=final_norm_w[None, :])


def _trunk(x, p, prm):
    b, s, _ = x.shape
    t = b * s
    x2d = x.reshape(t, D_MODEL)
    q, k, v, z, xbc, gates, dt = _in_proj(x2d, prm["norm1_w"], prm["w_all"])
    cos, sin_signed = _rope_tables(s)

    ts = min(KV_PREP_TILE, s)
    k_heads, vt_heads = _kv_prep(k.reshape(b, s, KV_DIM), v.reshape(b, s, KV_DIM), cos, sin_signed,
                                 prm["ones_blk"], prm["kw"], ts)
    attn = _attention(q.reshape(b, s, ATTN_DIM), k_heads, vt_heads, cos, sin_signed,
                      prm["ones_blk"], prm["qw"], min(ATTN_Q_TILE, s))

    xc = _xbc_conv(xbc.reshape(b, s, CONV_DIM), prm["shift_prev"], prm["shift_next"],
                   prm["conv_w"], prm["conv_b"])
    dt3 = dt.reshape(b, s, 2 * V7X_LANES)
    y_fwd = _ssd_pass(xc, dt3, (prm["dtb_f"], prm["a_f"], prm["tri_f"], prm["e_rows"]),
                      reverse=False)
    ssd = _ssd_pass(xc, dt3, (prm["dtb_b"], prm["a_b"], prm["tri_b"], prm["e_rows"],
                              prm["dskip"], prm["ssd_nw"]),
                    reverse=True, extra=(z.reshape(b, s, D_INNER), y_fwd))

    x1, h2 = _merge(x2d, attn.reshape(t, ATTN_DIM), ssd.reshape(t, D_INNER), gates,
                    prm["gate_b"], prm["wa"], prm["ws"], prm["wo"], prm["norm2_w"])
    u = _ffn_up(h2, prm["w_up"])
    out = _ffn_down(u, x1, p[0].reshape(t, PLE_DIM), s, prm["ffn_cw"], prm["ffn_cb"],
                    prm["w_down"], prm["w_ple"], prm["w_pg"], prm["b_pg"], prm["final_w"])
    return out.reshape(b, s, D_MODEL)


def kernel(x_prompt, x_sample, p_prompt, p_sample, norm1_w, w_in, ssm_conv_w, ssm_conv_b, dt_bias,
           a_log, d_skip, ssd_norm_w, q_norm_w, k_norm_w, gate_b, w_attn_branch, w_ssd_branch,
           w_out, norm2_w, w_up, ffn_conv_w, ffn_conv_b, w_down, w_ple, w_ple_gate, b_ple_gate,
           final_norm_w):
    prm = _prepare(norm1_w, w_in, ssm_conv_w, ssm_conv_b, dt_bias, a_log, d_skip, ssd_norm_w,
                   q_norm_w, k_norm_w, gate_b, w_attn_branch, w_ssd_branch, w_out, norm2_w, w_up,
                   ffn_conv_w, ffn_conv_b, w_down, w_ple, w_ple_gate, b_ple_gate, final_norm_w)
    return (_trunk(x_prompt, p_prompt, prm), _trunk(x_sample, p_sample, prm))
```

```python
import functools
import math

import jax
import jax.numpy as jnp
from jax import lax
from jax.experimental import pallas as pl
from jax.experimental.pallas import tpu as pltpu

F32 = jnp.float32
BF16 = jnp.bfloat16

D_MODEL = 1024
N_HEADS = 16
N_KV_HEADS = 4
HEAD_DIM = 64
GQA_GROUP = N_HEADS // N_KV_HEADS
ATTN_DIM = N_HEADS * HEAD_DIM
KV_DIM = N_KV_HEADS * HEAD_DIM
GROUP_Q_DIM = GQA_GROUP * HEAD_DIM
VT_ROWS = HEAD_DIM + 16
AXIS_ROT_DIM = HEAD_DIM // 2
ROPE_THETA = 10000.0
GRID_W = 64

D_INNER = 2048
SSM_HEADDIM = 64
N_SSM_HEADS = D_INNER // SSM_HEADDIM
N_SSM_GROUPS = 4
HEADS_PER_GROUP = N_SSM_HEADS // N_SSM_GROUPS
D_STATE = 128
GROUP_INNER = HEADS_PER_GROUP * SSM_HEADDIM
CONV_DIM = D_INNER + 2 * N_SSM_GROUPS * D_STATE
CHUNK = 128

D_FF = 2816
PLE_DIM = 256
NORM_EPS = 1e-6

V7X_LANES = 128
V7X_BF16_SUBLANES = 16
V7X_VMEM_LIMIT_BYTES = 60000 * 1024

MASKED_SEG = -1e30
LOG2E = 1.4426950408889634

ROW_TILE = 256
KV_PREP_TILE = 2048
ATTN_Q_TILE = 512
ATTN_KEY_SUB = 256
ATTN_Q_SLAB = 512
ATTN_PIPE_DEPTH = 3
FFN_ROW_TILE = 256
SSD_BATCH_ROWS = 2
HALO = V7X_BF16_SUBLANES


def _cparams(*sem):
    return pltpu.CompilerParams(dimension_semantics=sem, vmem_limit_bytes=V7X_VMEM_LIMIT_BYTES)


def _const_spec(shape):
    nd = len(shape)
    return pl.BlockSpec(shape, lambda *_: (0,) * nd)


def _dot(a, b):
    return jnp.dot(a, b, preferred_element_type=F32)


def _dot_nt(a, b):
    return lax.dot_general(a, b, (((1,), (1,)), ((), ())), preferred_element_type=F32)


def _rms_rows(x, w):
    return x * lax.rsqrt(jnp.mean(x * x, axis=-1, keepdims=True) + NORM_EPS) * w


def _sigmoid(x):
    return 1.0 / (1.0 + jnp.exp(-x))


_IN_PROJ_OUTS = (
    (ATTN_DIM, BF16), (KV_DIM, BF16), (KV_DIM, BF16), (D_INNER, BF16), (CONV_DIM, BF16),
    (2 * D_MODEL, BF16), (2 * V7X_LANES, F32))
_N_CHUNK = 512


def _in_proj_kernel(x_ref, nw_ref, w_ref, *out_refs):
    h = _rms_rows(x_ref[...], nw_ref[...]).astype(BF16)
    col = 0
    for o_ref, (width, _) in zip(out_refs, _IN_PROJ_OUTS):
        for c in range(0, width, _N_CHUNK):
            cw = min(_N_CHUNK, width - c)
            o_ref[:, c:c + cw] = _dot(h, w_ref[:, col + c:col + c + cw]).astype(o_ref.dtype)
        col += width


def _in_proj(x2d, norm_w, w_all):
    t = x2d.shape[0]
    tm = min(ROW_TILE, t)
    n_all = w_all.shape[1]
    return pl.pallas_call(
        _in_proj_kernel,
        grid=(t // tm,),
        in_specs=[pl.BlockSpec((tm, D_MODEL), lambda i: (i, 0)),
                  _const_spec((1, D_MODEL)),
                  _const_spec((D_MODEL, n_all))],
        out_specs=[pl.BlockSpec((tm, w), lambda i: (i, 0)) for w, _ in _IN_PROJ_OUTS],
        out_shape=[jax.ShapeDtypeStruct((t, w), dt) for w, dt in _IN_PROJ_OUTS],
        compiler_params=_cparams("parallel"),
        name="in_proj",
    )(x2d, norm_w, w_all)


def _head_norm_rope(x, ones_blk, w, cos, sin_signed):
    width = x.shape[-1]
    ss = _dot((x * x).astype(BF16), ones_blk)
    xn = x * lax.rsqrt(ss * (1.0 / HEAD_DIM) + NORM_EPS) * w
    lane = lax.broadcasted_iota(jnp.int32, xn.shape, 1)
    half = HEAD_DIM // 2
    partner = jnp.where((lane % HEAD_DIM) < half,
                        pltpu.roll(xn, width - half, 1), pltpu.roll(xn, half, 1))
    return xn * cos + partner * sin_signed


def _kv_prep_kernel(k_ref, v_ref, cos_ref, sin_ref, ones_ref, kw_ref, ko_ref, vto_ref):
    k = _head_norm_rope(k_ref[0].astype(F32), ones_ref[...], kw_ref[...],
                        cos_ref[...], sin_ref[...])
    vt = v_ref[0].astype(F32).T
    ts = vt.shape[1]
    sub = lax.broadcasted_iota(jnp.int32, (VT_ROWS - HEAD_DIM, ts), 0)
    tail = jnp.where(sub == 0, 1.0, 0.0).astype(BF16)
    for h in range(N_KV_HEADS):
        ko_ref[0, h] = k[:, h * HEAD_DIM:(h + 1) * HEAD_DIM].astype(BF16)
        vto_ref[0, h, 0, 0:HEAD_DIM, :] = vt[h * HEAD_DIM:(h + 1) * HEAD_DIM, :].astype(BF16)
        vto_ref[0, h, 0, HEAD_DIM:VT_ROWS, :] = tail


def _kv_prep(k, v, cos, sin_signed, ones_blk, kw, ts):
    b, s, _ = k.shape
    nk = s // ts
    return pl.pallas_call(
        _kv_prep_kernel,
        grid=(b, nk),
        in_specs=[pl.BlockSpec((1, ts, KV_DIM), lambda i, j: (i, j, 0)),
                  pl.BlockSpec((1, ts, KV_DIM), lambda i, j: (i, j, 0)),
                  pl.BlockSpec((ts, KV_DIM), lambda i, j: (j, 0)),
                  pl.BlockSpec((ts, KV_DIM), lambda i, j: (j, 0)),
                  _const_spec((KV_DIM, KV_DIM)),
                  _const_spec((1, KV_DIM))],
        out_specs=[pl.BlockSpec((1, N_KV_HEADS, ts, HEAD_DIM), lambda i, j: (i, 0, j, 0)),
                   pl.BlockSpec((1, N_KV_HEADS, 1, VT_ROWS, ts), lambda i, j: (i, 0, j, 0, 0))],
        out_shape=[jax.ShapeDtypeStruct((b, N_KV_HEADS, s, HEAD_DIM), BF16),
                   jax.ShapeDtypeStruct((b, N_KV_HEADS, nk, VT_ROWS, ts), BF16)],
        compiler_params=_cparams("parallel", "parallel"),
        name="kv_prep",
    )(k, v, cos, sin_signed, ones_blk, kw)


def _attn_kernel(q_ref, k_ref, vt_ref, cos_ref, sin_ref, ones_ref, qw_ref, o_ref,
                 qs_ref, m_ref, acc_ref, sp_ref, *, tq, tk, nk):
    q = _head_norm_rope(q_ref[0].astype(F32), ones_ref[...], qw_ref[...],
                        cos_ref[...], sin_ref[...])
    q_t = (q * (HEAD_DIM ** -0.5 * LOG2E)).T
    for h in range(GQA_GROUP):
        qs_ref[:, h * tq:(h + 1) * tq] = q_t[h * HEAD_DIM:(h + 1) * HEAD_DIM, :].astype(BF16)
    m_ref[...] = jnp.full(m_ref.shape, -jnp.inf, F32)
    acc_ref[...] = jnp.zeros(acc_ref.shape, F32)

    units = [(i, n) for i in range(tk // ATTN_KEY_SUB)
             for n in range(GQA_GROUP * tq // ATTN_Q_SLAB)]
    n_units = len(units)
    depth = min(ATTN_PIPE_DEPTH, n_units)

    def scores(j, unit):
        i, n = unit
        row0 = pl.multiple_of(j * tk + i * ATTN_KEY_SUB, ATTN_KEY_SUB)
        k = k_ref[0, 0, pl.ds(row0, ATTN_KEY_SUB), :]
        return _dot(k, qs_ref[:, n * ATTN_Q_SLAB:(n + 1) * ATTN_Q_SLAB])

    def softmax_pv(j, unit, s_t):
        i, n = unit
        cs = slice(n * ATTN_Q_SLAB, (n + 1) * ATTN_Q_SLAB)
        v_i = vt_ref[0, 0, j, :, i * ATTN_KEY_SUB:(i + 1) * ATTN_KEY_SUB]
        m_prev = m_ref[:, cs]
        s_b = s_t.astype(BF16)
        m_new = jnp.maximum(m_prev, jnp.max(s_b, axis=0, keepdims=True).astype(F32))
        p = jnp.exp2(s_b - m_new.astype(BF16))
        alpha = jnp.exp2(m_prev - m_new)
        acc_ref[:, cs] = alpha * acc_ref[:, cs] + _dot(v_i, p)
        m_ref[:, cs] = m_new

    for d in range(depth):
        sp_ref[d] = scores(0, units[d])

    def kv_step(j, carry):
        j_next = jnp.minimum(j + 1, nk - 1)
        pending = {}
        for idx, u in enumerate(units):
            ahead = idx + depth
            if ahead < n_units:
                pending[ahead] = scores(j, units[ahead])
            s_t = sp_ref[idx] if idx < depth else pending.pop(idx)
            softmax_pv(j, u, s_t)
            if ahead >= n_units:
                sp_ref[ahead - n_units] = scores(j_next, units[ahead - n_units])
        return carry

    lax.fori_loop(0, nk, kv_step, 0)
    out_t = acc_ref[0:HEAD_DIM, :] * (1.0 / acc_ref[HEAD_DIM:HEAD_DIM + 1, :])
    out_t = jnp.concatenate([out_t[:, h * tq:(h + 1) * tq] for h in range(GQA_GROUP)], axis=0)
    o_ref[0] = out_t.T.astype(o_ref.dtype)


def _attention(q, k_heads, vt_heads, cos, sin_signed, ones_blk, qw, tq):
    b, s, _ = q.shape
    nk, tk = vt_heads.shape[2], vt_heads.shape[4]
    vq = GQA_GROUP * tq
    kern = functools.partial(_attn_kernel, tq=tq, tk=tk, nk=nk)
    return pl.pallas_call(
        kern,
        grid=(b, N_KV_HEADS, s // tq),
        in_specs=[pl.BlockSpec((1, tq, GROUP_Q_DIM), lambda i, g, j: (i, j, g)),
                  pl.BlockSpec((1, 1, s, HEAD_DIM), lambda i, g, j: (i, g, 0, 0)),
                  pl.BlockSpec((1, 1, nk, VT_ROWS, tk), lambda i, g, j: (i, g, 0, 0, 0)),
                  pl.BlockSpec((tq, GROUP_Q_DIM), lambda i, g, j: (j, 0)),
                  pl.BlockSpec((tq, GROUP_Q_DIM), lambda i, g, j: (j, 0)),
                  _const_spec((GROUP_Q_DIM, GROUP_Q_DIM)),
                  _const_spec((1, GROUP_Q_DIM))],
        out_specs=pl.BlockSpec((1, tq, GROUP_Q_DIM), lambda i, g, j: (i, j, g)),
        out_shape=jax.ShapeDtypeStruct((b, s, ATTN_DIM), BF16),
        scratch_shapes=[pltpu.VMEM((HEAD_DIM, vq), BF16),
                        pltpu.VMEM((1, vq), F32),
                        pltpu.VMEM((VT_ROWS, vq), F32),
                        pltpu.VMEM((ATTN_PIPE_DEPTH, ATTN_KEY_SUB, ATTN_Q_SLAB), F32)],
        compiler_params=_cparams("parallel", "parallel", "arbitrary"),
        name="attention",
    )(q, k_heads, vt_heads, cos, sin_signed, ones_blk, qw)


HEAD_LANE_COPIES = V7X_LANES // N_SSM_HEADS


def _split_select(v):
    hi = v.astype(BF16)
    r1 = v - hi.astype(F32)
    mid = r1.astype(BF16)
    lo = (r1 - mid.astype(F32)).astype(BF16)
    copy = lax.broadcasted_iota(jnp.int32, v.shape, 1) // N_SSM_HEADS
    return jnp.where(copy == 0, hi, jnp.where(copy == 1, mid, jnp.where(copy == 2, lo, jnp.zeros_like(lo))))


def _expand_heads(v, e_ref):
    return _dot(_split_select(v), e_ref[...])


def _silu(x):
    return x * (1.0 / (1.0 + jnp.exp2(x * (-LOG2E))))


def _xbc_conv_kernel(main_ref, prev_ref, next_ref, sp_ref, sn_ref, cw_ref, cb_ref, o_ref, *, nc):
    cc = pl.program_id(1)
    prev_ok = (cc > 0).astype(BF16)
    next_ok = (cc < nc - 1).astype(BF16)
    for c0 in range(0, CONV_DIM, _N_CHUNK):
        cols = slice(c0, c0 + _N_CHUNK)
        main = main_ref[0, :, cols]
        ext = jnp.concatenate([prev_ref[0, :, cols] * prev_ok, main, next_ref[0, :, cols] * next_ok],
                              axis=0)
        conv = (_dot(sp_ref[...], ext) * cw_ref[0:1, cols] + main.astype(F32) * cw_ref[1:2, cols]
                + _dot(sn_ref[...], ext) * cw_ref[2:3, cols] + cb_ref[:, cols])
        o_ref[0, :, cols] = _silu(conv).astype(o_ref.dtype)


def _xbc_conv(xbc, shift_prev, shift_next, conv_w, conv_b):
    b, s, _ = xbc.shape
    nc = s // CHUNK
    nhalo = s // HALO
    per = CHUNK // HALO
    consts = (shift_prev, shift_next, conv_w, conv_b)
    return pl.pallas_call(
        functools.partial(_xbc_conv_kernel, nc=nc),
        grid=(b, nc),
        in_specs=[pl.BlockSpec((1, CHUNK, CONV_DIM), lambda i, c: (i, c, 0)),
                  pl.BlockSpec((1, HALO, CONV_DIM), lambda i, c: (i, jnp.maximum(c * per - 1, 0), 0)),
                  pl.BlockSpec((1, HALO, CONV_DIM),
                               lambda i, c: (i, jnp.minimum((c + 1) * per, nhalo - 1), 0))]
                 + [_const_spec(c.shape) for c in consts],
        out_specs=pl.BlockSpec((1, CHUNK, CONV_DIM), lambda i, c: (i, c, 0)),
        out_shape=jax.ShapeDtypeStruct((b, s, CONV_DIM), BF16),
        compiler_params=_cparams("parallel", "parallel"),
        name="xbc_conv",
    )(xbc, xbc, xbc, *consts)


def _ssd_kernel(*refs, reverse):
    if reverse:
        (xc_ref, dt_ref, z_ref, yf_ref, dtb_ref, a_ref, tri_ref, e_ref, dskip_ref, nw_ref,
         o_ref, h_ref) = refs
    else:
        xc_ref, dt_ref, dtb_ref, a_ref, tri_ref, e_ref, o_ref, h_ref = refs
    @pl.when(pl.program_id(1) == 0)
    def _():
        h_ref[...] = jnp.zeros(h_ref.shape, F32)

    for r in range(xc_ref.shape[0]):
        _ssd_chunk(r, refs, reverse)


def _ssd_chunk(r, refs, reverse):
    if reverse:
        (xc_ref, dt_ref, z_ref, yf_ref, dtb_ref, a_ref, tri_ref, e_ref, dskip_ref, nw_ref,
         o_ref, h_ref) = refs
    else:
        xc_ref, dt_ref, dtb_ref, a_ref, tri_ref, e_ref, o_ref, h_ref = refs
    L = CHUNK
    nh = N_SSM_HEADS
    x = xc_ref[r, :, 0:D_INNER].astype(F32)

    dt_raw = dt_ref[r] + dtb_ref[...]
    dtv = jnp.maximum(dt_raw, 0.0) + jnp.log(1.0 + jnp.exp(-jnp.abs(dt_raw)))
    da = dtv * a_ref[...]

    tri = tri_ref[...]
    da_sel = _split_select(da)
    part = _dot(tri, da_sel)
    cum = part
    for cpy in range(1, HEAD_LANE_COPIES):
        cum = cum + pltpu.roll(part, cpy * nh, 1)
    part_t = lax.dot_general(da_sel, tri, (((0,), (1,)), ((), ())), preferred_element_type=F32)
    cum_t = part_t[:nh] + part_t[nh:2 * nh] + part_t[2 * nh:3 * nh]
    last = 0 if reverse else L - 1
    tot = cum[last:last + 1, :]

    dt_e = _expand_heads(dtv, e_ref)
    grow_e = _expand_heads(jnp.exp(cum), e_ref)
    tail_e = _expand_heads(dtv * jnp.exp(tot - cum), e_ref)
    xdt = (x * dt_e).astype(BF16)
    xtail = (x * tail_e).astype(BF16)
    chunk_decay = grow_e[last:last + 1, :]

    li = lax.broadcasted_iota(jnp.int32, (L, L), 0)
    si = lax.broadcasted_iota(jnp.int32, (L, L), 1)
    mask = (si >= li) if reverse else (si <= li)
    lane = lax.broadcasted_iota(jnp.int32, (L, 2 * SSM_HEADDIM), 1)
    first_head = lane < SSM_HEADDIM

    y_parts = []
    for g in range(N_SSM_GROUPS):
        b0 = D_INNER + g * D_STATE
        c0 = D_INNER + (N_SSM_GROUPS + g) * D_STATE
        b_bf = xc_ref[r, :, b0:b0 + D_STATE]
        c_bf = xc_ref[r, :, c0:c0 + D_STATE]
        cb = _dot_nt(c_bf, b_bf)
        gs = slice(g * GROUP_INNER, (g + 1) * GROUP_INNER)
        h_in = h_ref[r, g]
        y_off = _dot(c_bf, h_in.astype(BF16)) * grow_e[:, gs]
        b_t = b_bf.astype(F32).T.astype(BF16)
        h_ref[r, g] = h_in * chunk_decay[:, gs] + _dot(b_t, xtail[:, gs])
        diag = []
        for pr in range(HEADS_PER_GROUP // 2):
            ms = []
            for hh in (g * HEADS_PER_GROUP + 2 * pr, g * HEADS_PER_GROUP + 2 * pr + 1):
                seg = cum[:, hh:hh + 1] - cum_t[hh:hh + 1, :]
                dec = jnp.exp(jnp.where(mask, seg, MASKED_SEG))
                ms.append((cb * dec).astype(BF16))
            lhs = jnp.concatenate(ms, axis=1)
            xp = xdt[:, g * GROUP_INNER + pr * 128:g * GROUP_INNER + (pr + 1) * 128]
            zero = jnp.zeros_like(xp)
            rhs = jnp.concatenate([jnp.where(first_head, xp, zero),
                                   jnp.where(first_head, zero, xp)], axis=0)
            diag.append(_dot(lhs, rhs))
        y_parts.append(jnp.concatenate(diag, axis=1) + y_off)
    y = jnp.concatenate(y_parts, axis=1)

    if not reverse:
        o_ref[r] = y.astype(o_ref.dtype)
    else:
        y = y + yf_ref[r].astype(F32) + x * dskip_ref[...]
        y = y * _silu(z_ref[r].astype(F32))
        outs = []
        for g in range(N_SSM_GROUPS):
            yg = y[:, g * GROUP_INNER:(g + 1) * GROUP_INNER]
            outs.append(yg * lax.rsqrt(jnp.mean(yg * yg, axis=-1, keepdims=True) + NORM_EPS))
        o_ref[r] = (jnp.concatenate(outs, axis=1) * nw_ref[...]).astype(o_ref.dtype)


def _ssd_pass(xc, dt, consts, reverse, extra=()):
    b, s, _ = xc.shape
    nc = s // CHUNK
    rows = math.gcd(b, SSD_BATCH_ROWS)
    chunk_map = (lambda i, c: (i, nc - 1 - c, 0)) if reverse else (lambda i, c: (i, c, 0))
    dt_map = (lambda i, c: (i, nc - 1 - c, 1)) if reverse else (lambda i, c: (i, c, 0))
    in_specs = [pl.BlockSpec((rows, CHUNK, CONV_DIM), chunk_map),
                pl.BlockSpec((rows, CHUNK, V7X_LANES), dt_map)]
    args = [xc, dt]
    for e in extra:
        in_specs.append(pl.BlockSpec((rows, CHUNK, D_INNER), chunk_map))
        args.append(e)
    for cst in consts:
        in_specs.append(_const_spec(cst.shape))
        args.append(cst)
    return pl.pallas_call(
        functools.partial(_ssd_kernel, reverse=reverse),
        grid=(b // rows, nc),
        in_specs=in_specs,
        out_specs=pl.BlockSpec((rows, CHUNK, D_INNER), chunk_map),
        out_shape=jax.ShapeDtypeStruct((b, s, D_INNER), BF16),
        scratch_shapes=[pltpu.VMEM((rows, N_SSM_GROUPS, D_STATE, GROUP_INNER), F32)],
        compiler_params=_cparams("parallel", "arbitrary"),
        name="ssd_bwd" if reverse else "ssd_fwd",
    )(*args)


def _merge_kernel(x_ref, attn_ref, ssd_ref, gate_ref, gb_ref, wa_ref, ws_ref, wo_ref, nw_ref,
                  x1_ref, h2_ref):
    g = _sigmoid(gate_ref[...].astype(F32) + gb_ref[...])
    merged = (g[:, :D_MODEL] * _dot(attn_ref[...], wa_ref[...])
              + g[:, D_MODEL:] * _dot(ssd_ref[...], ws_ref[...]))
    x1 = x_ref[...] + _dot(merged.astype(BF16), wo_ref[...])
    x1_ref[...] = x1
    h2_ref[...] = _rms_rows(x1, nw_ref[...]).astype(BF16)


def _merge(x2d, attn, ssd, gates, gate_b, wa, ws, wo, norm2_w):
    t = x2d.shape[0]
    tm = min(ROW_TILE, t)
    row = lambda w: pl.BlockSpec((tm, w), lambda i: (i, 0))
    return pl.pallas_call(
        _merge_kernel,
        grid=(t // tm,),
        in_specs=[row(D_MODEL), row(ATTN_DIM), row(D_INNER), row(2 * D_MODEL),
                  _const_spec(gate_b.shape), _const_spec(wa.shape), _const_spec(ws.shape),
                  _const_spec(wo.shape), _const_spec(norm2_w.shape)],
        out_specs=[row(D_MODEL), row(D_MODEL)],
        out_shape=[jax.ShapeDtypeStruct((t, D_MODEL), F32),
                   jax.ShapeDtypeStruct((t, D_MODEL), BF16)],
        compiler_params=_cparams("parallel"),
        name="merge",
    )(x2d, attn, ssd, gates, gate_b, wa, ws, wo, norm2_w)


def _ffn_up_kernel(h_ref, w_ref, u_ref):
    h = h_ref[...]
    n = w_ref.shape[1]
    for c in range(0, n, _N_CHUNK):
        u_ref[:, c:c + _N_CHUNK] = _dot(h, w_ref[:, c:c + _N_CHUNK]).astype(u_ref.dtype)


def _ffn_up(h2, w_up):
    t = h2.shape[0]
    tm = min(ROW_TILE, t)
    n = w_up.shape[1]
    return pl.pallas_call(
        _ffn_up_kernel,
        grid=(t // tm,),
        in_specs=[pl.BlockSpec((tm, D_MODEL), lambda i: (i, 0)), _const_spec(w_up.shape)],
        out_specs=pl.BlockSpec((tm, n), lambda i: (i, 0)),
        out_shape=jax.ShapeDtypeStruct((t, n), BF16),
        compiler_params=_cparams("parallel"),
        name="ffn_up",
    )(h2, w_up)


def _gelu_tanh(x):
    return 0.5 * x * (1.0 + jnp.tanh(math.sqrt(2.0 / math.pi) * (x + 0.044715 * (x * x * x))))


def _ffn_down_kernel(u_ref, prev_ref, next_ref, x1_ref, p_ref, cw_ref, cb_ref, wd_ref, wple_ref,
                     wpg_ref, bpg_ref, fw_ref, o_ref, us_ref, *, tm, tiles_per_seq):
    i = pl.program_id(0)
    j = i % tiles_per_seq
    prev_ok = (j > 0).astype(F32)
    next_ok = (j < tiles_per_seq - 1).astype(F32)
    base = 8
    us_ref[base:base + tm, :] = u_ref[...].astype(F32)
    us_ref[base - 1:base, :] = prev_ref[HALO - 1:HALO, :].astype(F32) * prev_ok
    us_ref[base + tm:base + tm + 1, :] = next_ref[0:1, :].astype(F32) * next_ok

    def conv(cols):
        return (us_ref[base - 1:base - 1 + tm, cols] * cw_ref[0:1, cols]
                + us_ref[base:base + tm, cols] * cw_ref[1:2, cols]
                + us_ref[base + 1:base + 1 + tm, cols] * cw_ref[2:3, cols] + cb_ref[:, cols])

    act = (_gelu_tanh(conv(slice(0, D_FF))) * conv(slice(D_FF, 2 * D_FF))).astype(BF16)
    x2 = x1_ref[...] + _dot(act, wd_ref[...])
    pg = _sigmoid(_dot(x2.astype(BF16), wpg_ref[...]) + bpg_ref[...])
    x3 = x2 + pg * _dot(p_ref[...].astype(BF16), wple_ref[...])
    o_ref[...] = _rms_rows(x3, fw_ref[...])


def _ffn_down(u, x1, p2d, seq, conv_w, conv_b, w_down, w_ple, w_pg, b_pg, final_w):
    t = u.shape[0]
    tm = min(FFN_ROW_TILE, seq)
    tiles_per_seq = seq // tm
    per = tm // HALO
    nhalo = t // HALO
    n = u.shape[1]
    kern = functools.partial(_ffn_down_kernel, tm=tm, tiles_per_seq=tiles_per_seq)
    consts = (conv_w, conv_b, w_down, w_ple, w_pg, b_pg, final_w)
    return pl.pallas_call(
        kern,
        grid=(t // tm,),
        in_specs=[pl.BlockSpec((tm, n), lambda i: (i, 0)),
                  pl.BlockSpec((HALO, n), lambda i: (jnp.maximum(i * per - 1, 0), 0)),
                  pl.BlockSpec((HALO, n), lambda i: (jnp.minimum((i + 1) * per, nhalo - 1), 0)),
                  pl.BlockSpec((tm, D_MODEL), lambda i: (i, 0)),
                  pl.BlockSpec((tm, PLE_DIM), lambda i: (i, 0))]
                 + [_const_spec(c.shape) for c in consts],
        out_specs=pl.BlockSpec((tm, D_MODEL), lambda i: (i, 0)),
        out_shape=jax.ShapeDtypeStruct((t, D_MODEL), F32),
        scratch_shapes=[pltpu.VMEM((tm + 16, n), F32)],
        compiler_params=_cparams("parallel"),
        name="ffn_down",
    )(u, u, u, x1, p2d, *consts)


def _rope_tables(n_tokens):
    rows = n_tokens // GRID_W
    row_idx = jnp.repeat(jnp.arange(rows, dtype=F32), GRID_W)
    col_idx = jnp.tile(jnp.arange(GRID_W, dtype=F32), rows)
    inv_freq = ROPE_THETA ** (-jnp.arange(0, AXIS_ROT_DIM, 2, dtype=F32) / AXIS_ROT_DIM)
    ang = jnp.concatenate([row_idx[:, None] * inv_freq, col_idx[:, None] * inv_freq], axis=-1)
    cos, sin = jnp.cos(ang), jnp.sin(ang)
    cos_h = jnp.concatenate([cos, cos], axis=-1)
    sin_h = jnp.concatenate([-sin, sin], axis=-1)
    return jnp.tile(cos_h, (1, N_KV_HEADS)), jnp.tile(sin_h, (1, N_KV_HEADS))


def _prepare(norm1_w, w_in, ssm_conv_w, ssm_conv_b, dt_bias, a_log, d_skip, ssd_norm_w,
             q_norm_w, k_norm_w, gate_b, w_attn_branch, w_ssd_branch, w_out, norm2_w, w_up,
             ffn_conv_w, ffn_conv_b, w_down, w_ple, w_ple_gate, b_ple_gate, final_norm_w):
    perm = jnp.concatenate([jnp.arange(0, HEAD_DIM, 2), jnp.arange(1, HEAD_DIM, 2)])
    w = w_in[0]
    sizes = (ATTN_DIM, KV_DIM, KV_DIM, D_INNER, CONV_DIM, 2 * N_SSM_HEADS, 2 * D_MODEL)
    parts, start = [], 0
    for sz in sizes:
        parts.append(w[:, start:start + sz])
        start += sz
    wq, wk, wv, wz, wxbc, wdt, wg = parts
    wq = wq.reshape(D_MODEL, N_HEADS, HEAD_DIM)[:, :, perm].reshape(D_MODEL, ATTN_DIM)
    wk = wk.reshape(D_MODEL, N_KV_HEADS, HEAD_DIM)[:, :, perm].reshape(D_MODEL, KV_DIM)
    wdt = jnp.concatenate([wdt[:, :N_SSM_HEADS]] * HEAD_LANE_COPIES
                          + [wdt[:, N_SSM_HEADS:]] * HEAD_LANE_COPIES, axis=1)
    w_all = jnp.concatenate([wq, wk, wv, wz, wxbc, wg, wdt], axis=1).astype(BF16)

    head_id = jnp.arange(KV_DIM) // HEAD_DIM
    ones_blk = (head_id[:, None] == head_id[None, :]).astype(BF16)
    li = jnp.arange(CHUNK)
    tri_fwd = (li[None, :] <= li[:, None]).astype(BF16)
    tri_bwd = (li[None, :] >= li[:, None]).astype(BF16)
    chan_head = jnp.arange(D_INNER) // SSM_HEADDIM
    e1 = (jnp.arange(N_SSM_HEADS)[:, None] == chan_head[None, :]).astype(BF16)
    e_rows = jnp.concatenate([e1] * (HEAD_LANE_COPIES - 1) + [jnp.zeros_like(e1)], axis=0)
    ext = jnp.arange(CHUNK + 2 * HALO)
    shift_prev = (ext[None, :] == li[:, None] + HALO - 1).astype(BF16)
    shift_next = (ext[None, :] == li[:, None] + HALO + 1).astype(BF16)
    a = jnp.tile(-jnp.exp(a_log[0].astype(F32)), (1, HEAD_LANE_COPIES))
    dtb = jnp.tile(dt_bias[0], (1, HEAD_LANE_COPIES))
    return dict(
        norm1_w=norm1_w[0][None, :], w_all=w_all,
        ones_blk=ones_blk,
        qw=jnp.tile(q_norm_w[0][perm], GQA_GROUP)[None, :],
        kw=jnp.tile(k_norm_w[0][perm], N_KV_HEADS)[None, :],
        conv_w=ssm_conv_w[0], conv_b=ssm_conv_b[0][None, :],
        dtb_f=dtb[0][None, :], dtb_b=dtb[1][None, :],
        a_f=a[0][None, :], a_b=a[1][None, :],
        tri_f=tri_fwd, tri_b=tri_bwd, e_rows=e_rows,
        shift_prev=shift_prev, shift_next=shift_next,
        dskip=jnp.repeat(d_skip[0], SSM_HEADDIM)[None, :], ssd_nw=ssd_norm_w[0][None, :],
        gate_b=gate_b[0][None, :], wa=w_attn_branch[0].astype(BF16),
        ws=w_ssd_branch[0].astype(BF16), wo=w_out[0].astype(BF16), norm2_w=norm2_w[0][None, :],
        w_up=w_up[0].astype(BF16), ffn_cw=ffn_conv_w[0], ffn_cb=ffn_conv_b[0][None, :],
        w_down=w_down[0].astype(BF16), w_ple=w_ple[0].astype(BF16),
        w_pg=w_ple_gate[0].astype(BF16), b_pg=b_ple_gate[0][None, :],
        final_w=final_norm_w[None, :])


def _trunk(x, p, prm):
    b, s, _ = x.shape
    t = b * s
    x2d = x.reshape(t, D_MODEL)
    q, k, v, z, xbc, gates, dt = _in_proj(x2d, prm["norm1_w"], prm["w_all"])
    cos, sin_signed = _rope_tables(s)

    ts = min(KV_PREP_TILE, s)
    k_heads, vt_heads = _kv_prep(k.reshape(b, s, KV_DIM), v.reshape(b, s, KV_DIM), cos, sin_signed,
                                 prm["ones_blk"], prm["kw"], ts)
    attn = _attention(q.reshape(b, s, ATTN_DIM), k_heads, vt_heads, cos, sin_signed,
                      prm["ones_blk"], prm["qw"], min(ATTN_Q_TILE, s))

    xc = _xbc_conv(xbc.reshape(b, s, CONV_DIM), prm["shift_prev"], prm["shift_next"],
                   prm["conv_w"], prm["conv_b"])
    dt3 = dt.reshape(b, s, 2 * V7X_LANES)
    y_fwd = _ssd_pass(xc, dt3, (prm["dtb_f"], prm["a_f"], prm["tri_f"], prm["e_rows"]),
                      reverse=False)
    ssd = _ssd_pass(xc, dt3, (prm["dtb_b"], prm["a_b"], prm["tri_b"], prm["e_rows"],
                              prm["dskip"], prm["ssd_nw"]),
                    reverse=True, extra=(z.reshape(b, s, D_INNER), y_fwd))

    x1, h2 = _merge(x2d, attn.reshape(t, ATTN_DIM), ssd.reshape(t, D_INNER), gates,
                    prm["gate_b"], prm["wa"], prm["ws"], prm["wo"], prm["norm2_w"])
    u = _ffn_up(h2, prm["w_up"])
    out = _ffn_down(u, x1, p[0].reshape(t, PLE_DIM), s, prm["ffn_cw"], prm["ffn_cb"],
                    prm["w_down"], prm["w_ple"], prm["w_pg"], prm["b_pg"], prm["final_w"])
    return out.reshape(b, s, D_MODEL)


def kernel(x_prompt, x_sample, p_prompt, p_sample, norm1_w, w_in, ssm_conv_w, ssm_conv_b, dt_bias,
           a_log, d_skip, ssd_norm_w, q_norm_w, k_norm_w, gate_b, w_attn_branch, w_ssd_branch,
           w_out, norm2_w, w_up, ffn_conv_w, ffn_conv_b, w_down, w_ple, w_ple_gate, b_ple_gate,
           final_norm_w):
    prm = _prepare(norm1_w, w_in, ssm_conv_w, ssm_conv_b, dt_bias, a_log, d_skip, ssd_norm_w,
                   q_norm_w, k_norm_w, gate_b, w_attn_branch, w_ssd_branch, w_out, norm2_w, w_up,
                   ffn_conv_w, ffn_conv_b, w_down, w_ple, w_ple_gate, b_ple_gate, final_norm_w)
    return (_trunk(x_prompt, p_prompt, prm), _trunk(x_sample, p_sample, prm))
```

```python
import functools
import math

import jax
import jax.numpy as jnp
from jax import lax
from jax.experimental import pallas as pl
from jax.experimental.pallas import tpu as pltpu

F32 = jnp.float32
BF16 = jnp.bfloat16

D_MODEL = 1024
N_HEADS = 16
N_KV_HEADS = 4
HEAD_DIM = 64
GQA_GROUP = N_HEADS // N_KV_HEADS
ATTN_DIM = N_HEADS * HEAD_DIM
KV_DIM = N_KV_HEADS * HEAD_DIM
GROUP_Q_DIM = GQA_GROUP * HEAD_DIM
VT_ROWS = HEAD_DIM + 16
AXIS_ROT_DIM = HEAD_DIM // 2
ROPE_THETA = 10000.0
GRID_W = 64

D_INNER = 2048
SSM_HEADDIM = 64
N_SSM_HEADS = D_INNER // SSM_HEADDIM
N_SSM_GROUPS = 4
HEADS_PER_GROUP = N_SSM_HEADS // N_SSM_GROUPS
D_STATE = 128
GROUP_INNER = HEADS_PER_GROUP * SSM_HEADDIM
CONV_DIM = D_INNER + 2 * N_SSM_GROUPS * D_STATE
CHUNK = 128

D_FF = 2816
PLE_DIM = 256
NORM_EPS = 1e-6

V7X_LANES = 128
V7X_BF16_SUBLANES = 16
V7X_VMEM_LIMIT_BYTES = 60000 * 1024

MASKED_SEG = -1e30
LOG2E = 1.4426950408889634

ROW_TILE = 256
KV_PREP_TILE = 1024
ATTN_Q_TILE = 1024
ATTN_KEY_SUB = 256
ATTN_Q_SLAB = 512
ATTN_PIPE_DEPTH = 3
FFN_ROW_TILE = 256
SSD_BATCH_ROWS = 2
HALO = V7X_BF16_SUBLANES


def _cparams(*sem):
    return pltpu.CompilerParams(dimension_semantics=sem, vmem_limit_bytes=V7X_VMEM_LIMIT_BYTES)


def _const_spec(shape):
    nd = len(shape)
    return pl.BlockSpec(shape, lambda *_: (0,) * nd)


def _dot(a, b):
    return jnp.dot(a, b, preferred_element_type=F32)


def _dot_nt(a, b):
    return lax.dot_general(a, b, (((1,), (1,)), ((), ())), preferred_element_type=F32)


def _rms_rows(x, w):
    return x * lax.rsqrt(jnp.mean(x * x, axis=-1, keepdims=True) + NORM_EPS) * w


def _sigmoid(x):
    return 1.0 / (1.0 + jnp.exp(-x))


_IN_PROJ_OUTS = (
    (ATTN_DIM, BF16), (KV_DIM, BF16), (KV_DIM, BF16), (D_INNER, BF16), (CONV_DIM, BF16),
    (2 * D_MODEL, BF16), (2 * V7X_LANES, F32))
_N_CHUNK = 512


def _in_proj_kernel(x_ref, nw_ref, w_ref, *out_refs):
    h = _rms_rows(x_ref[...], nw_ref[...]).astype(BF16)
    col = 0
    for o_ref, (width, _) in zip(out_refs, _IN_PROJ_OUTS):
        for c in range(0, width, _N_CHUNK):
            cw = min(_N_CHUNK, width - c)
            o_ref[:, c:c + cw] = _dot(h, w_ref[:, col + c:col + c + cw]).astype(o_ref.dtype)
        col += width


def _in_proj(x2d, norm_w, w_all):
    t = x2d.shape[0]
    tm = min(ROW_TILE, t)
    n_all = w_all.shape[1]
    return pl.pallas_call(
        _in_proj_kernel,
        grid=(t // tm,),
        in_specs=[pl.BlockSpec((tm, D_MODEL), lambda i: (i, 0)),
                  _const_spec((1, D_MODEL)),
                  _const_spec((D_MODEL, n_all))],
        out_specs=[pl.BlockSpec((tm, w), lambda i: (i, 0)) for w, _ in _IN_PROJ_OUTS],
        out_shape=[jax.ShapeDtypeStruct((t, w), dt) for w, dt in _IN_PROJ_OUTS],
        compiler_params=_cparams("parallel"),
        name="in_proj",
    )(x2d, norm_w, w_all)


def _head_norm_rope(x, ones_blk, w, cos, sin_signed):
    width = x.shape[-1]
    ss = _dot((x * x).astype(BF16), ones_blk)
    xn = x * lax.rsqrt(ss * (1.0 / HEAD_DIM) + NORM_EPS) * w
    lane = lax.broadcasted_iota(jnp.int32, xn.shape, 1)
    half = HEAD_DIM // 2
    partner = jnp.where((lane % HEAD_DIM) < half,
                        pltpu.roll(xn, width - half, 1), pltpu.roll(xn, half, 1))
    return xn * cos + partner * sin_signed


def _kv_prep_kernel(k_ref, v_ref, cos_ref, sin_ref, ones_ref, kw_ref, ko_ref, vto_ref):
    k = _head_norm_rope(k_ref[0].astype(F32), ones_ref[...], kw_ref[...],
                        cos_ref[...], sin_ref[...])
    vt = v_ref[0].astype(F32).T
    ts = vt.shape[1]
    sub = lax.broadcasted_iota(jnp.int32, (VT_ROWS - HEAD_DIM, ts), 0)
    tail = jnp.where(sub == 0, 1.0, 0.0).astype(BF16)
    for h in range(N_KV_HEADS):
        ko_ref[0, h] = k[:, h * HEAD_DIM:(h + 1) * HEAD_DIM].astype(BF16)
        vto_ref[0, h, 0, 0:HEAD_DIM, :] = vt[h * HEAD_DIM:(h + 1) * HEAD_DIM, :].astype(BF16)
        vto_ref[0, h, 0, HEAD_DIM:VT_ROWS, :] = tail


def _kv_prep(k, v, cos, sin_signed, ones_blk, kw, ts):
    b, s, _ = k.shape
    nk = s // ts
    return pl.pallas_call(
        _kv_prep_kernel,
        grid=(b, nk),
        in_specs=[pl.BlockSpec((1, ts, KV_DIM), lambda i, j: (i, j, 0)),
                  pl.BlockSpec((1, ts, KV_DIM), lambda i, j: (i, j, 0)),
                  pl.BlockSpec((ts, KV_DIM), lambda i, j: (j, 0)),
                  pl.BlockSpec((ts, KV_DIM), lambda i, j: (j, 0)),
                  _const_spec((KV_DIM, KV_DIM)),
                  _const_spec((1, KV_DIM))],
        out_specs=[pl.BlockSpec((1, N_KV_HEADS, ts, HEAD_DIM), lambda i, j: (i, 0, j, 0)),
                   pl.BlockSpec((1, N_KV_HEADS, 1, VT_ROWS, ts), lambda i, j: (i, 0, j, 0, 0))],
        out_shape=[jax.ShapeDtypeStruct((b, N_KV_HEADS, s, HEAD_DIM), BF16),
                   jax.ShapeDtypeStruct((b, N_KV_HEADS, nk, VT_ROWS, ts), BF16)],
        compiler_params=_cparams("parallel", "parallel"),
        name="kv_prep",
    )(k, v, cos, sin_signed, ones_blk, kw)


def _attn_kernel(q_ref, k_ref, vt_ref, cos_ref, sin_ref, ones_ref, qw_ref, o_ref,
                 qs_ref, m_ref, acc_ref, sp_ref, *, tq, tk, nk):
    q = _head_norm_rope(q_ref[0].astype(F32), ones_ref[...], qw_ref[...],
                        cos_ref[...], sin_ref[...])
    q_t = (q * (HEAD_DIM ** -0.5 * LOG2E)).T
    for h in range(GQA_GROUP):
        qs_ref[:, h * tq:(h + 1) * tq] = q_t[h * HEAD_DIM:(h + 1) * HEAD_DIM, :].astype(BF16)
    m_ref[...] = jnp.full(m_ref.shape, -jnp.inf, F32)
    acc_ref[...] = jnp.zeros(acc_ref.shape, F32)

    units = [(i, n) for i in range(tk // ATTN_KEY_SUB)
             for n in range(GQA_GROUP * tq // ATTN_Q_SLAB)]
    n_units = len(units)
    depth = min(ATTN_PIPE_DEPTH, n_units)

    def scores(j, unit):
        i, n = unit
        row0 = pl.multiple_of(j * tk + i * ATTN_KEY_SUB, ATTN_KEY_SUB)
        k = k_ref[0, 0, pl.ds(row0, ATTN_KEY_SUB), :]
        return _dot(k, qs_ref[:, n * ATTN_Q_SLAB:(n + 1) * ATTN_Q_SLAB])

    def softmax_pv(j, unit, s_t):
        i, n = unit
        cs = slice(n * ATTN_Q_SLAB, (n + 1) * ATTN_Q_SLAB)
        v_i = vt_ref[0, 0, j, :, i * ATTN_KEY_SUB:(i + 1) * ATTN_KEY_SUB]
        m_prev = m_ref[:, cs]
        s_b = s_t.astype(BF16)
        m_new = jnp.maximum(m_prev, jnp.max(s_b, axis=0, keepdims=True).astype(F32))
        p = jnp.exp2(s_b - m_new.astype(BF16))
        alpha = jnp.exp2(m_prev - m_new)
        acc_ref[:, cs] = alpha * acc_ref[:, cs] + _dot(v_i, p)
        m_ref[:, cs] = m_new

    for d in range(depth):
        sp_ref[d] = scores(0, units[d])

    def kv_step(j, carry):
        j_next = jnp.minimum(j + 1, nk - 1)
        pending = {}
        for idx, u in enumerate(units):
            ahead = idx + depth
            if ahead < n_units:
                pending[ahead] = scores(j, units[ahead])
            s_t = sp_ref[idx] if idx < depth else pending.pop(idx)
            softmax_pv(j, u, s_t)
            if ahead >= n_units:
                sp_ref[ahead - n_units] = scores(j_next, units[ahead - n_units])
        return carry

    lax.fori_loop(0, nk, kv_step, 0)
    out_t = acc_ref[0:HEAD_DIM, :] * (1.0 / acc_ref[HEAD_DIM:HEAD_DIM + 1, :])
    out_t = jnp.concatenate([out_t[:, h * tq:(h + 1) * tq] for h in range(GQA_GROUP)], axis=0)
    o_ref[0] = out_t.T.astype(o_ref.dtype)


def _attention(q, k_heads, vt_heads, cos, sin_signed, ones_blk, qw, tq):
    b, s, _ = q.shape
    nk, tk = vt_heads.shape[2], vt_heads.shape[4]
    vq = GQA_GROUP * tq
    kern = functools.partial(_attn_kernel, tq=tq, tk=tk, nk=nk)
    return pl.pallas_call(
        kern,
        grid=(b, N_KV_HEADS, s // tq),
        in_specs=[pl.BlockSpec((1, tq, GROUP_Q_DIM), lambda i, g, j: (i, j, g)),
                  pl.BlockSpec((1, 1, s, HEAD_DIM), lambda i, g, j: (i, g, 0, 0)),
                  pl.BlockSpec((1, 1, nk, VT_ROWS, tk), lambda i, g, j: (i, g, 0, 0, 0)),
                  pl.BlockSpec((tq, GROUP_Q_DIM), lambda i, g, j: (j, 0)),
                  pl.BlockSpec((tq, GROUP_Q_DIM), lambda i, g, j: (j, 0)),
                  _const_spec((GROUP_Q_DIM, GROUP_Q_DIM)),
                  _const_spec((1, GROUP_Q_DIM))],
        out_specs=pl.BlockSpec((1, tq, GROUP_Q_DIM), lambda i, g, j: (i, j, g)),
        out_shape=jax.ShapeDtypeStruct((b, s, ATTN_DIM), BF16),
        scratch_shapes=[pltpu.VMEM((HEAD_DIM, vq), BF16),
                        pltpu.VMEM((1, vq), F32),
                        pltpu.VMEM((VT_ROWS, vq), F32),
                        pltpu.VMEM((ATTN_PIPE_DEPTH, ATTN_KEY_SUB, ATTN_Q_SLAB), F32)],
        compiler_params=_cparams("parallel", "parallel", "arbitrary"),
        name="attention",
    )(q, k_heads, vt_heads, cos, sin_signed, ones_blk, qw)


HEAD_LANE_COPIES = V7X_LANES // N_SSM_HEADS


def _split_select(v):
    hi = v.astype(BF16)
    r1 = v - hi.astype(F32)
    mid = r1.astype(BF16)
    lo = (r1 - mid.astype(F32)).astype(BF16)
    copy = lax.broadcasted_iota(jnp.int32, v.shape, 1) // N_SSM_HEADS
    return jnp.where(copy == 0, hi, jnp.where(copy == 1, mid, jnp.where(copy == 2, lo, jnp.zeros_like(lo))))


def _expand_heads(v, e_ref):
    return _dot(_split_select(v), e_ref[...])


def _silu(x):
    return x * (1.0 / (1.0 + jnp.exp2(x * (-LOG2E))))


def _xbc_conv_kernel(main_ref, prev_ref, next_ref, sp_ref, sn_ref, cw_ref, cb_ref, o_ref, *, nc):
    cc = pl.program_id(1)
    prev_ok = (cc > 0).astype(BF16)
    next_ok = (cc < nc - 1).astype(BF16)
    for c0 in range(0, CONV_DIM, _N_CHUNK):
        cols = slice(c0, c0 + _N_CHUNK)
        main = main_ref[0, :, cols]
        ext = jnp.concatenate([prev_ref[0, :, cols] * prev_ok, main, next_ref[0, :, cols] * next_ok],
                              axis=0)
        conv = (_dot(sp_ref[...], ext) * cw_ref[0:1, cols] + main.astype(F32) * cw_ref[1:2, cols]
                + _dot(sn_ref[...], ext) * cw_ref[2:3, cols] + cb_ref[:, cols])
        o_ref[0, :, cols] = _silu(conv).astype(o_ref.dtype)


def _xbc_conv(xbc, shift_prev, shift_next, conv_w, conv_b):
    b, s, _ = xbc.shape
    nc = s // CHUNK
    nhalo = s // HALO
    per = CHUNK // HALO
    consts = (shift_prev, shift_next, conv_w, conv_b)
    return pl.pallas_call(
        functools.partial(_xbc_conv_kernel, nc=nc),
        grid=(b, nc),
        in_specs=[pl.BlockSpec((1, CHUNK, CONV_DIM), lambda i, c: (i, c, 0)),
                  pl.BlockSpec((1, HALO, CONV_DIM), lambda i, c: (i, jnp.maximum(c * per - 1, 0), 0)),
                  pl.BlockSpec((1, HALO, CONV_DIM),
                               lambda i, c: (i, jnp.minimum((c + 1) * per, nhalo - 1), 0))]
                 + [_const_spec(c.shape) for c in consts],
        out_specs=pl.BlockSpec((1, CHUNK, CONV_DIM), lambda i, c: (i, c, 0)),
        out_shape=jax.ShapeDtypeStruct((b, s, CONV_DIM), BF16),
        compiler_params=_cparams("parallel", "parallel"),
        name="xbc_conv",
    )(xbc, xbc, xbc, *consts)


def _ssd_kernel(*refs, reverse):
    if reverse:
        (xc_ref, dt_ref, z_ref, yf_ref, dtb_ref, a_ref, tri_ref, e_ref, dskip_ref, nw_ref,
         o_ref, h_ref) = refs
    else:
        xc_ref, dt_ref, dtb_ref, a_ref, tri_ref, e_ref, o_ref, h_ref = refs
    @pl.when(pl.program_id(1) == 0)
    def _():
        h_ref[...] = jnp.zeros(h_ref.shape, F32)

    for r in range(xc_ref.shape[0]):
        _ssd_chunk(r, refs, reverse)


def _ssd_chunk(r, refs, reverse):
    if reverse:
        (xc_ref, dt_ref, z_ref, yf_ref, dtb_ref, a_ref, tri_ref, e_ref, dskip_ref, nw_ref,
         o_ref, h_ref) = refs
    else:
        xc_ref, dt_ref, dtb_ref, a_ref, tri_ref, e_ref, o_ref, h_ref = refs
    L = CHUNK
    nh = N_SSM_HEADS
    x = xc_ref[r, :, 0:D_INNER].astype(F32)

    dt_raw = dt_ref[r] + dtb_ref[...]
    dtv = jnp.maximum(dt_raw, 0.0) + jnp.log(1.0 + jnp.exp(-jnp.abs(dt_raw)))
    da = dtv * a_ref[...]

    tri = tri_ref[...]
    da_sel = _split_select(da)
    part = _dot(tri, da_sel)
    cum = part
    for cpy in range(1, HEAD_LANE_COPIES):
        cum = cum + pltpu.roll(part, cpy * nh, 1)
    part_t = lax.dot_general(da_sel, tri, (((0,), (1,)), ((), ())), preferred_element_type=F32)
    cum_t = part_t[:nh] + part_t[nh:2 * nh] + part_t[2 * nh:3 * nh]
    last = 0 if reverse else L - 1
    tot = cum[last:last + 1, :]

    dt_e = _expand_heads(dtv, e_ref)
    grow_e = _expand_heads(jnp.exp(cum), e_ref)
    tail_e = _expand_heads(dtv * jnp.exp(tot - cum), e_ref)
    xdt = (x * dt_e).astype(BF16)
    xtail = (x * tail_e).astype(BF16)
    chunk_decay = grow_e[last:last + 1, :]

    li = lax.broadcasted_iota(jnp.int32, (L, L), 0)
    si = lax.broadcasted_iota(jnp.int32, (L, L), 1)
    mask = (si >= li) if reverse else (si <= li)
    lane = lax.broadcasted_iota(jnp.int32, (L, 2 * SSM_HEADDIM), 1)
    first_head = lane < SSM_HEADDIM

    y_parts = []
    for g in range(N_SSM_GROUPS):
        b0 = D_INNER + g * D_STATE
        c0 = D_INNER + (N_SSM_GROUPS + g) * D_STATE
        b_bf = xc_ref[r, :, b0:b0 + D_STATE]
        c_bf = xc_ref[r, :, c0:c0 + D_STATE]
        cb = _dot_nt(c_bf, b_bf)
        gs = slice(g * GROUP_INNER, (g + 1) * GROUP_INNER)
        h_in = h_ref[r, g]
        y_off = _dot(c_bf, h_in.astype(BF16)) * grow_e[:, gs]
        b_t = b_bf.astype(F32).T.astype(BF16)
        h_ref[r, g] = h_in * chunk_decay[:, gs] + _dot(b_t, xtail[:, gs])
        diag = []
        for pr in range(HEADS_PER_GROUP // 2):
            ms = []
            for hh in (g * HEADS_PER_GROUP + 2 * pr, g * HEADS_PER_GROUP + 2 * pr + 1):
                seg = cum[:, hh:hh + 1] - cum_t[hh:hh + 1, :]
                dec = jnp.exp(jnp.where(mask, seg, MASKED_SEG))
                ms.append((cb * dec).astype(BF16))
            lhs = jnp.concatenate(ms, axis=1)
            xp = xdt[:, g * GROUP_INNER + pr * 128:g * GROUP_INNER + (pr + 1) * 128]
            zero = jnp.zeros_like(xp)
            rhs = jnp.concatenate([jnp.where(first_head, xp, zero),
                                   jnp.where(first_head, zero, xp)], axis=0)
            diag.append(_dot(lhs, rhs))
        y_parts.append(jnp.concatenate(diag, axis=1) + y_off)
    y = jnp.concatenate(y_parts, axis=1)

    if not reverse:
        o_ref[r] = y.astype(o_ref.dtype)
    else:
        y = y + yf_ref[r].astype(F32) + x * dskip_ref[...]
        y = y * _silu(z_ref[r].astype(F32))
        outs = []
        for g in range(N_SSM_GROUPS):
            yg = y[:, g * GROUP_INNER:(g + 1) * GROUP_INNER]
            outs.append(yg * lax.rsqrt(jnp.mean(yg * yg, axis=-1, keepdims=True) + NORM_EPS))
        o_ref[r] = (jnp.concatenate(outs, axis=1) * nw_ref[...]).astype(o_ref.dtype)


def _ssd_pass(xc, dt, consts, reverse, extra=()):
    b, s, _ = xc.shape
    nc = s // CHUNK
    rows = math.gcd(b, SSD_BATCH_ROWS)
    chunk_map = (lambda i, c: (i, nc - 1 - c, 0)) if reverse else (lambda i, c: (i, c, 0))
    dt_map = (lambda i, c: (i, nc - 1 - c, 1)) if reverse else (lambda i, c: (i, c, 0))
    in_specs = [pl.BlockSpec((rows, CHUNK, CONV_DIM), chunk_map),
                pl.BlockSpec((rows, CHUNK, V7X_LANES), dt_map)]
    args = [xc, dt]
    for e in extra:
        in_specs.append(pl.BlockSpec((rows, CHUNK, D_INNER), chunk_map))
        args.append(e)
    for cst in consts:
        in_specs.append(_const_spec(cst.shape))
        args.append(cst)
    return pl.pallas_call(
        functools.partial(_ssd_kernel, reverse=reverse),
        grid=(b // rows, nc),
        in_specs=in_specs,
        out_specs=pl.BlockSpec((rows, CHUNK, D_INNER), chunk_map),
        out_shape=jax.ShapeDtypeStruct((b, s, D_INNER), BF16),
        scratch_shapes=[pltpu.VMEM((rows, N_SSM_GROUPS, D_STATE, GROUP_INNER), F32)],
        compiler_params=_cparams("parallel", "arbitrary"),
        name="ssd_bwd" if reverse else "ssd_fwd",
    )(*args)


def _merge_kernel(x_ref, attn_ref, ssd_ref, gate_ref, gb_ref, wa_ref, ws_ref, wo_ref, nw_ref,
                  x1_ref, h2_ref):
    g = _sigmoid(gate_ref[...].astype(F32) + gb_ref[...])
    merged = (g[:, :D_MODEL] * _dot(attn_ref[...], wa_ref[...])
              + g[:, D_MODEL:] * _dot(ssd_ref[...], ws_ref[...]))
    x1 = x_ref[...] + _dot(merged.astype(BF16), wo_ref[...])
    x1_ref[...] = x1
    h2_ref[...] = _rms_rows(x1, nw_ref[...]).astype(BF16)


def _merge(x2d, attn, ssd, gates, gate_b, wa, ws, wo, norm2_w):
    t = x2d.shape[0]
    tm = min(ROW_TILE, t)
    row = lambda w: pl.BlockSpec((tm, w), lambda i: (i, 0))
    return pl.pallas_call(
        _merge_kernel,
        grid=(t // tm,),
        in_specs=[row(D_MODEL), row(ATTN_DIM), row(D_INNER), row(2 * D_MODEL),
                  _const_spec(gate_b.shape), _const_spec(wa.shape), _const_spec(ws.shape),
                  _const_spec(wo.shape), _const_spec(norm2_w.shape)],
        out_specs=[row(D_MODEL), row(D_MODEL)],
        out_shape=[jax.ShapeDtypeStruct((t, D_MODEL), F32),
                   jax.ShapeDtypeStruct((t, D_MODEL), BF16)],
        compiler_params=_cparams("parallel"),
        name="merge",
    )(x2d, attn, ssd, gates, gate_b, wa, ws, wo, norm2_w)


def _ffn_up_kernel(h_ref, w_ref, u_ref):
    h = h_ref[...]
    n = w_ref.shape[1]
    for c in range(0, n, _N_CHUNK):
        u_ref[:, c:c + _N_CHUNK] = _dot(h, w_ref[:, c:c + _N_CHUNK]).astype(u_ref.dtype)


def _ffn_up(h2, w_up):
    t = h2.shape[0]
    tm = min(ROW_TILE, t)
    n = w_up.shape[1]
    return pl.pallas_call(
        _ffn_up_kernel,
        grid=(t // tm,),
        in_specs=[pl.BlockSpec((tm, D_MODEL), lambda i: (i, 0)), _const_spec(w_up.shape)],
        out_specs=pl.BlockSpec((tm, n), lambda i: (i, 0)),
        out_shape=jax.ShapeDtypeStruct((t, n), BF16),
        compiler_params=_cparams("parallel"),
        name="ffn_up",
    )(h2, w_up)


_FFN_COL_CHUNK = 256


def _gelu_tanh(x):
    return 0.5 * x * (1.0 + jnp.tanh(math.sqrt(2.0 / math.pi) * (x + 0.044715 * (x * x * x))))


def _ffn_down_kernel(u_ref, prev_ref, next_ref, x1_ref, p_ref, sp_ref, sn_ref, cw_ref, cb_ref,
                     wd_ref, wple_ref, wpg_ref, bpg_ref, fw_ref, o_ref, act_ref, *, tm, tiles_per_seq):
    j = pl.program_id(0) % tiles_per_seq
    prev_ok = (j > 0).astype(BF16)
    next_ok = (j < tiles_per_seq - 1).astype(BF16)
    sp = sp_ref[...]
    sn = sn_ref[...]

    def conv(r0, cols):
        main = u_ref[r0:r0 + CHUNK, cols]
        before = prev_ref[:, cols] * prev_ok if r0 == 0 else u_ref[r0 - HALO:r0, cols]
        last = r0 + CHUNK == tm
        after = next_ref[:, cols] * next_ok if last else u_ref[r0 + CHUNK:r0 + CHUNK + HALO, cols]
        ext = jnp.concatenate([before, main, after], axis=0)
        return (_dot(sp, ext) * cw_ref[0:1, cols] + main.astype(F32) * cw_ref[1:2, cols]
                + _dot(sn, ext) * cw_ref[2:3, cols] + cb_ref[:, cols])

    for r0 in range(0, tm, CHUNK):
        for c0 in range(0, D_FF, _FFN_COL_CHUNK):
            gate = conv(r0, slice(c0, c0 + _FFN_COL_CHUNK))
            val = conv(r0, slice(D_FF + c0, D_FF + c0 + _FFN_COL_CHUNK))
            act_ref[r0:r0 + CHUNK, c0:c0 + _FFN_COL_CHUNK] = (_gelu_tanh(gate) * val).astype(BF16)

    x2 = x1_ref[...] + _dot(act_ref[...], wd_ref[...])
    pg = _sigmoid(_dot(x2.astype(BF16), wpg_ref[...]) + bpg_ref[...])
    x3 = x2 + pg * _dot(p_ref[...].astype(BF16), wple_ref[...])
    o_ref[...] = _rms_rows(x3, fw_ref[...])


def _ffn_down(u, x1, p2d, seq, shift_prev, shift_next, conv_w, conv_b, w_down, w_ple, w_pg, b_pg,
              final_w):
    t = u.shape[0]
    tm = min(FFN_ROW_TILE, seq)
    tiles_per_seq = seq // tm
    per = tm // HALO
    nhalo = t // HALO
    n = u.shape[1]
    kern = functools.partial(_ffn_down_kernel, tm=tm, tiles_per_seq=tiles_per_seq)
    consts = (shift_prev, shift_next, conv_w, conv_b, w_down, w_ple, w_pg, b_pg, final_w)
    return pl.pallas_call(
        kern,
        grid=(t // tm,),
        in_specs=[pl.BlockSpec((tm, n), lambda i: (i, 0)),
                  pl.BlockSpec((HALO, n), lambda i: (jnp.maximum(i * per - 1, 0), 0)),
                  pl.BlockSpec((HALO, n), lambda i: (jnp.minimum((i + 1) * per, nhalo - 1), 0)),
                  pl.BlockSpec((tm, D_MODEL), lambda i: (i, 0)),
                  pl.BlockSpec((tm, PLE_DIM), lambda i: (i, 0))]
                 + [_const_spec(c.shape) for c in consts],
        out_specs=pl.BlockSpec((tm, D_MODEL), lambda i: (i, 0)),
        out_shape=jax.ShapeDtypeStruct((t, D_MODEL), F32),
        scratch_shapes=[pltpu.VMEM((tm, D_FF), BF16)],
        compiler_params=_cparams("parallel"),
        name="ffn_down",
    )(u, u, u, x1, p2d, *consts)


def _rope_tables(n_tokens):
    rows = n_tokens // GRID_W
    row_idx = jnp.repeat(jnp.arange(rows, dtype=F32), GRID_W)
    col_idx = jnp.tile(jnp.arange(GRID_W, dtype=F32), rows)
    inv_freq = ROPE_THETA ** (-jnp.arange(0, AXIS_ROT_DIM, 2, dtype=F32) / AXIS_ROT_DIM)
    ang = jnp.concatenate([row_idx[:, None] * inv_freq, col_idx[:, None] * inv_freq], axis=-1)
    cos, sin = jnp.cos(ang), jnp.sin(ang)
    cos_h = jnp.concatenate([cos, cos], axis=-1)
    sin_h = jnp.concatenate([-sin, sin], axis=-1)
    return jnp.tile(cos_h, (1, N_KV_HEADS)), jnp.tile(sin_h, (1, N_KV_HEADS))


def _prepare(norm1_w, w_in, ssm_conv_w, ssm_conv_b, dt_bias, a_log, d_skip, ssd_norm_w,
             q_norm_w, k_norm_w, gate_b, w_attn_branch, w_ssd_branch, w_out, norm2_w, w_up,
             ffn_conv_w, ffn_conv_b, w_down, w_ple, w_ple_gate, b_ple_gate, final_norm_w):
    perm = jnp.concatenate([jnp.arange(0, HEAD_DIM, 2), jnp.arange(1, HEAD_DIM, 2)])
    w = w_in[0]
    sizes = (ATTN_DIM, KV_DIM, KV_DIM, D_INNER, CONV_DIM, 2 * N_SSM_HEADS, 2 * D_MODEL)
    parts, start = [], 0
    for sz in sizes:
        parts.append(w[:, start:start + sz])
        start += sz
    wq, wk, wv, wz, wxbc, wdt, wg = parts
    wq = wq.reshape(D_MODEL, N_HEADS, HEAD_DIM)[:, :, perm].reshape(D_MODEL, ATTN_DIM)
    wk = wk.reshape(D_MODEL, N_KV_HEADS, HEAD_DIM)[:, :, perm].reshape(D_MODEL, KV_DIM)
    wdt = jnp.concatenate([wdt[:, :N_SSM_HEADS]] * HEAD_LANE_COPIES
                          + [wdt[:, N_SSM_HEADS:]] * HEAD_LANE_COPIES, axis=1)
    w_all = jnp.concatenate([wq, wk, wv, wz, wxbc, wg, wdt], axis=1).astype(BF16)

    head_id = jnp.arange(KV_DIM) // HEAD_DIM
    ones_blk = (head_id[:, None] == head_id[None, :]).astype(BF16)
    li = jnp.arange(CHUNK)
    tri_fwd = (li[None, :] <= li[:, None]).astype(BF16)
    tri_bwd = (li[None, :] >= li[:, None]).astype(BF16)
    chan_head = jnp.arange(D_INNER) // SSM_HEADDIM
    e1 = (jnp.arange(N_SSM_HEADS)[:, None] == chan_head[None, :]).astype(BF16)
    e_rows = jnp.concatenate([e1] * (HEAD_LANE_COPIES - 1) + [jnp.zeros_like(e1)], axis=0)
    ext = jnp.arange(CHUNK + 2 * HALO)
    shift_prev = (ext[None, :] == li[:, None] + HALO - 1).astype(BF16)
    shift_next = (ext[None, :] == li[:, None] + HALO + 1).astype(BF16)
    a = jnp.tile(-jnp.exp(a_log[0].astype(F32)), (1, HEAD_LANE_COPIES))
    dtb = jnp.tile(dt_bias[0], (1, HEAD_LANE_COPIES))
    return dict(
        norm1_w=norm1_w[0][None, :], w_all=w_all,
        ones_blk=ones_blk,
        qw=jnp.tile(q_norm_w[0][perm], GQA_GROUP)[None, :],
        kw=jnp.tile(k_norm_w[0][perm], N_KV_HEADS)[None, :],
        conv_w=ssm_conv_w[0], conv_b=ssm_conv_b[0][None, :],
        dtb_f=dtb[0][None, :], dtb_b=dtb[1][None, :],
        a_f=a[0][None, :], a_b=a[1][None, :],
        tri_f=tri_fwd, tri_b=tri_bwd, e_rows=e_rows,
        shift_prev=shift_prev, shift_next=shift_next,
        dskip=jnp.repeat(d_skip[0], SSM_HEADDIM)[None, :], ssd_nw=ssd_norm_w[0][None, :],
        gate_b=gate_b[0][None, :], wa=w_attn_branch[0].astype(BF16),
        ws=w_ssd_branch[0].astype(BF16), wo=w_out[0].astype(BF16), norm2_w=norm2_w[0][None, :],
        w_up=w_up[0].astype(BF16), ffn_cw=ffn_conv_w[0], ffn_cb=ffn_conv_b[0][None, :],
        w_down=w_down[0].astype(BF16), w_ple=w_ple[0].astype(BF16),
        w_pg=w_ple_gate[0].astype(BF16), b_pg=b_ple_gate[0][None, :],
        final_w=final_norm_w[None, :])


def _trunk(x, p, prm):
    b, s, _ = x.shape
    t = b * s
    x2d = x.reshape(t, D_MODEL)
    q, k, v, z, xbc, gates, dt = _in_proj(x2d, prm["norm1_w"], prm["w_all"])
    cos, sin_signed = _rope_tables(s)

    ts = min(KV_PREP_TILE, s)
    k_heads, vt_heads = _kv_prep(k.reshape(b, s, KV_DIM), v.reshape(b, s, KV_DIM), cos, sin_signed,
                                 prm["ones_blk"], prm["kw"], ts)
    attn = _attention(q.reshape(b, s, ATTN_DIM), k_heads, vt_heads, cos, sin_signed,
                      prm["ones_blk"], prm["qw"], min(ATTN_Q_TILE, s))

    xc = _xbc_conv(xbc.reshape(b, s, CONV_DIM), prm["shift_prev"], prm["shift_next"],
                   prm["conv_w"], prm["conv_b"])
    dt3 = dt.reshape(b, s, 2 * V7X_LANES)
    y_fwd = _ssd_pass(xc, dt3, (prm["dtb_f"], prm["a_f"], prm["tri_f"], prm["e_rows"]),
                      reverse=False)
    ssd = _ssd_pass(xc, dt3, (prm["dtb_b"], prm["a_b"], prm["tri_b"], prm["e_rows"],
                              prm["dskip"], prm["ssd_nw"]),
                    reverse=True, extra=(z.reshape(b, s, D_INNER), y_fwd))

    x1, h2 = _merge(x2d, attn.reshape(t, ATTN_DIM), ssd.reshape(t, D_INNER), gates,
                    prm["gate_b"], prm["wa"], prm["ws"], prm["wo"], prm["norm2_w"])
    u = _ffn_up(h2, prm["w_up"])
    out = _ffn_down(u, x1, p[0].reshape(t, PLE_DIM), s, prm["shift_prev"], prm["shift_next"],
                    prm["ffn_cw"], prm["ffn_cb"], prm["w_down"], prm["w_ple"], prm["w_pg"],
                    prm["b_pg"], prm["final_w"])
    return out.reshape(b, s, D_MODEL)


def kernel(x_prompt, x_sample, p_prompt, p_sample, norm1_w, w_in, ssm_conv_w, ssm_conv_b, dt_bias,
           a_log, d_skip, ssd_norm_w, q_norm_w, k_norm_w, gate_b, w_attn_branch, w_ssd_branch,
           w_out, norm2_w, w_up, ffn_conv_w, ffn_conv_b, w_down, w_ple, w_ple_gate, b_ple_gate,
           final_norm_w):
    prm = _prepare(norm1_w, w_in, ssm_conv_w, ssm_conv_b, dt_bias, a_log, d_skip, ssd_norm_w,
                   q_norm_w, k_norm_w, gate_b, w_attn_branch, w_ssd_branch, w_out, norm2_w, w_up,
                   ffn_conv_w, ffn_conv_b, w_down, w_ple, w_ple_gate, b_ple_gate, final_norm_w)
    return (_trunk(x_prompt, p_prompt, prm), _trunk(x_sample, p_sample, prm))
```

```python
import functools
import math

import jax
import jax.numpy as jnp
from jax import lax
from jax.experimental import pallas as pl
from jax.experimental.pallas import tpu as pltpu

F32 = jnp.float32
BF16 = jnp.bfloat16

D_MODEL = 1024
N_HEADS = 16
N_KV_HEADS = 4
HEAD_DIM = 64
GQA_GROUP = N_HEADS // N_KV_HEADS
ATTN_DIM = N_HEADS * HEAD_DIM
KV_DIM = N_KV_HEADS * HEAD_DIM
GROUP_Q_DIM = GQA_GROUP * HEAD_DIM
VT_ROWS = HEAD_DIM + 16
AXIS_ROT_DIM = HEAD_DIM // 2
ROPE_THETA = 10000.0
GRID_W = 64

D_INNER = 2048
SSM_HEADDIM = 64
N_SSM_HEADS = D_INNER // SSM_HEADDIM
N_SSM_GROUPS = 4
HEADS_PER_GROUP = N_SSM_HEADS // N_SSM_GROUPS
D_STATE = 128
GROUP_INNER = HEADS_PER_GROUP * SSM_HEADDIM
CONV_DIM = D_INNER + 2 * N_SSM_GROUPS * D_STATE
CHUNK = 128

D_FF = 2816
PLE_DIM = 256
NORM_EPS = 1e-6

V7X_LANES = 128
V7X_BF16_SUBLANES = 16
V7X_VMEM_LIMIT_BYTES = 60000 * 1024

MASKED_SEG = -1e30
LOG2E = 1.4426950408889634

ROW_TILE = 512
KV_PREP_TILE = 4096
ATTN_Q_TILE = 512
ATTN_KEY_SUB = 256
ATTN_Q_SLAB = 512
ATTN_PIPE_DEPTH = 4
FFN_ROW_TILE = 256
SSD_BATCH_ROWS = 2
HALO = V7X_BF16_SUBLANES


def _cparams(*sem):
    return pltpu.CompilerParams(dimension_semantics=sem, vmem_limit_bytes=V7X_VMEM_LIMIT_BYTES)


def _const_spec(shape):
    nd = len(shape)
    return pl.BlockSpec(shape, lambda *_: (0,) * nd, pipeline_mode=pl.Buffered(1))


def _dot(a, b):
    return jnp.dot(a, b, preferred_element_type=F32)


def _dot_nt(a, b):
    return lax.dot_general(a, b, (((1,), (1,)), ((), ())), preferred_element_type=F32)


def _rms_rows(x, w):
    return x * lax.rsqrt(jnp.mean(x * x, axis=-1, keepdims=True) + NORM_EPS) * w


def _sigmoid(x):
    return 1.0 / (1.0 + jnp.exp(-x))


_IN_PROJ_OUTS = (
    (ATTN_DIM, BF16), (KV_DIM, BF16), (KV_DIM, BF16), (D_INNER, BF16), (CONV_DIM, BF16),
    (2 * D_MODEL, BF16), (2 * V7X_LANES, F32))
_N_CHUNK = 512


def _in_proj_kernel(x_ref, nw_ref, w_ref, *out_refs):
    h = _rms_rows(x_ref[...], nw_ref[...]).astype(BF16)
    col = 0
    for o_ref, (width, _) in zip(out_refs, _IN_PROJ_OUTS):
        for c in range(0, width, _N_CHUNK):
            cw = min(_N_CHUNK, width - c)
            o_ref[:, c:c + cw] = _dot(h, w_ref[:, col + c:col + c + cw]).astype(o_ref.dtype)
        col += width


def _in_proj(x2d, norm_w, w_all):
    t = x2d.shape[0]
    tm = min(ROW_TILE, t)
    n_all = w_all.shape[1]
    return pl.pallas_call(
        _in_proj_kernel,
        grid=(t // tm,),
        in_specs=[pl.BlockSpec((tm, D_MODEL), lambda i: (i, 0)),
                  _const_spec((1, D_MODEL)),
                  _const_spec((D_MODEL, n_all))],
        out_specs=[pl.BlockSpec((tm, w), lambda i: (i, 0)) for w, _ in _IN_PROJ_OUTS],
        out_shape=[jax.ShapeDtypeStruct((t, w), dt) for w, dt in _IN_PROJ_OUTS],
        compiler_params=_cparams("parallel"),
        name="in_proj",
    )(x2d, norm_w, w_all)


def _head_norm_rope(x, ones_blk, w, cos, sin_signed):
    width = x.shape[-1]
    ss = _dot((x * x).astype(BF16), ones_blk)
    xn = x * lax.rsqrt(ss * (1.0 / HEAD_DIM) + NORM_EPS) * w
    lane = lax.broadcasted_iota(jnp.int32, xn.shape, 1)
    half = HEAD_DIM // 2
    partner = jnp.where((lane % HEAD_DIM) < half,
                        pltpu.roll(xn, width - half, 1), pltpu.roll(xn, half, 1))
    return xn * cos + partner * sin_signed


def _kv_prep_kernel(k_ref, v_ref, cos_ref, sin_ref, ones_ref, kw_ref, ko_ref, vto_ref):
    k = _head_norm_rope(k_ref[0].astype(F32), ones_ref[...], kw_ref[...],
                        cos_ref[...], sin_ref[...])
    vt = v_ref[0].astype(F32).T
    ts = vt.shape[1]
    sub = lax.broadcasted_iota(jnp.int32, (VT_ROWS - HEAD_DIM, ts), 0)
    tail = jnp.where(sub == 0, 1.0, 0.0).astype(BF16)
    for h in range(N_KV_HEADS):
        ko_ref[0, h] = k[:, h * HEAD_DIM:(h + 1) * HEAD_DIM].astype(BF16)
        vto_ref[0, h, 0, 0:HEAD_DIM, :] = vt[h * HEAD_DIM:(h + 1) * HEAD_DIM, :].astype(BF16)
        vto_ref[0, h, 0, HEAD_DIM:VT_ROWS, :] = tail


def _kv_prep(k, v, cos, sin_signed, ones_blk, kw, ts):
    b, s, _ = k.shape
    nk = s // ts
    return pl.pallas_call(
        _kv_prep_kernel,
        grid=(b, nk),
        in_specs=[pl.BlockSpec((1, ts, KV_DIM), lambda i, j: (i, j, 0)),
                  pl.BlockSpec((1, ts, KV_DIM), lambda i, j: (i, j, 0)),
                  pl.BlockSpec((ts, KV_DIM), lambda i, j: (j, 0)),
                  pl.BlockSpec((ts, KV_DIM), lambda i, j: (j, 0)),
                  _const_spec((KV_DIM, KV_DIM)),
                  _const_spec((1, KV_DIM))],
        out_specs=[pl.BlockSpec((1, N_KV_HEADS, ts, HEAD_DIM), lambda i, j: (i, 0, j, 0)),
                   pl.BlockSpec((1, N_KV_HEADS, 1, VT_ROWS, ts), lambda i, j: (i, 0, j, 0, 0))],
        out_shape=[jax.ShapeDtypeStruct((b, N_KV_HEADS, s, HEAD_DIM), BF16),
                   jax.ShapeDtypeStruct((b, N_KV_HEADS, nk, VT_ROWS, ts), BF16)],
        compiler_params=_cparams("parallel", "parallel"),
        name="kv_prep",
    )(k, v, cos, sin_signed, ones_blk, kw)


def _attn_kernel(q_ref, k_ref, vt_ref, cos_ref, sin_ref, ones_ref, qw_ref, o_ref,
                 qs_ref, m_ref, acc_ref, sp_ref, *, tq, tk, nk):
    q = _head_norm_rope(q_ref[0].astype(F32), ones_ref[...], qw_ref[...],
                        cos_ref[...], sin_ref[...])
    q_t = (q * (HEAD_DIM ** -0.5 * LOG2E)).T
    for h in range(GQA_GROUP):
        qs_ref[:, h * tq:(h + 1) * tq] = q_t[h * HEAD_DIM:(h + 1) * HEAD_DIM, :].astype(BF16)
    m_ref[...] = jnp.full(m_ref.shape, -jnp.inf, F32)
    acc_ref[...] = jnp.zeros(acc_ref.shape, F32)

    units = [(i, n) for i in range(tk // ATTN_KEY_SUB)
             for n in range(GQA_GROUP * tq // ATTN_Q_SLAB)]
    n_units = len(units)
    depth = min(ATTN_PIPE_DEPTH, n_units)

    def scores(j, unit):
        i, n = unit
        row0 = pl.multiple_of(j * tk + i * ATTN_KEY_SUB, ATTN_KEY_SUB)
        k = k_ref[0, 0, pl.ds(row0, ATTN_KEY_SUB), :]
        return _dot(k, qs_ref[:, n * ATTN_Q_SLAB:(n + 1) * ATTN_Q_SLAB])

    def softmax_pv(j, unit, s_t):
        i, n = unit
        cs = slice(n * ATTN_Q_SLAB, (n + 1) * ATTN_Q_SLAB)
        v_i = vt_ref[0, 0, j, :, i * ATTN_KEY_SUB:(i + 1) * ATTN_KEY_SUB]
        m_prev = m_ref[:, cs]
        s_b = s_t.astype(BF16)
        m_new = jnp.maximum(m_prev, jnp.max(s_b, axis=0, keepdims=True).astype(F32))
        p = jnp.exp2(s_b - m_new.astype(BF16))
        alpha = jnp.exp2(m_prev - m_new)
        acc_ref[:, cs] = alpha * acc_ref[:, cs] + _dot(v_i, p)
        m_ref[:, cs] = m_new

    for d in range(depth):
        sp_ref[d] = scores(0, units[d])

    def kv_step(j, carry):
        j_next = jnp.minimum(j + 1, nk - 1)
        pending = {}
        for idx, u in enumerate(units):
            ahead = idx + depth
            if ahead < n_units:
                pending[ahead] = scores(j, units[ahead])
            s_t = sp_ref[idx] if idx < depth else pending.pop(idx)
            softmax_pv(j, u, s_t)
            if ahead >= n_units:
                sp_ref[ahead - n_units] = scores(j_next, units[ahead - n_units])
        return carry

    lax.fori_loop(0, nk, kv_step, 0)
    out_t = acc_ref[0:HEAD_DIM, :] * (1.0 / acc_ref[HEAD_DIM:HEAD_DIM + 1, :])
    out_t = jnp.concatenate([out_t[:, h * tq:(h + 1) * tq] for h in range(GQA_GROUP)], axis=0)
    o_ref[0] = out_t.T.astype(o_ref.dtype)


def _attention(q, k_heads, vt_heads, cos, sin_signed, ones_blk, qw, tq):
    b, s, _ = q.shape
    nk, tk = vt_heads.shape[2], vt_heads.shape[4]
    vq = GQA_GROUP * tq
    kern = functools.partial(_attn_kernel, tq=tq, tk=tk, nk=nk)
    return pl.pallas_call(
        kern,
        grid=(b, N_KV_HEADS, s // tq),
        in_specs=[pl.BlockSpec((1, tq, GROUP_Q_DIM), lambda i, g, j: (i, j, g)),
                  pl.BlockSpec((1, 1, s, HEAD_DIM), lambda i, g, j: (i, g, 0, 0)),
                  pl.BlockSpec((1, 1, nk, VT_ROWS, tk), lambda i, g, j: (i, g, 0, 0, 0)),
                  pl.BlockSpec((tq, GROUP_Q_DIM), lambda i, g, j: (j, 0)),
                  pl.BlockSpec((tq, GROUP_Q_DIM), lambda i, g, j: (j, 0)),
                  _const_spec((GROUP_Q_DIM, GROUP_Q_DIM)),
                  _const_spec((1, GROUP_Q_DIM))],
        out_specs=pl.BlockSpec((1, tq, GROUP_Q_DIM), lambda i, g, j: (i, j, g)),
        out_shape=jax.ShapeDtypeStruct((b, s, ATTN_DIM), BF16),
        scratch_shapes=[pltpu.VMEM((HEAD_DIM, vq), BF16),
                        pltpu.VMEM((1, vq), F32),
                        pltpu.VMEM((VT_ROWS, vq), F32),
                        pltpu.VMEM((ATTN_PIPE_DEPTH, ATTN_KEY_SUB, ATTN_Q_SLAB), F32)],
        compiler_params=_cparams("parallel", "parallel", "arbitrary"),
        name="attention",
    )(q, k_heads, vt_heads, cos, sin_signed, ones_blk, qw)


HEAD_LANE_COPIES = V7X_LANES // N_SSM_HEADS


def _split_select(v):
    hi = v.astype(BF16)
    r1 = v - hi.astype(F32)
    mid = r1.astype(BF16)
    lo = (r1 - mid.astype(F32)).astype(BF16)
    copy = lax.broadcasted_iota(jnp.int32, v.shape, 1) // N_SSM_HEADS
    return jnp.where(copy == 0, hi, jnp.where(copy == 1, mid, jnp.where(copy == 2, lo, jnp.zeros_like(lo))))


def _expand_heads(v, e_ref):
    return _dot(_split_select(v), e_ref[...])


def _silu(x):
    return x * (1.0 / (1.0 + jnp.exp2(x * (-LOG2E))))


def _xbc_conv_kernel(main_ref, prev_ref, next_ref, sp_ref, sn_ref, cw_ref, cb_ref, o_ref, *, nc):
    cc = pl.program_id(1)
    prev_ok = (cc > 0).astype(BF16)
    next_ok = (cc < nc - 1).astype(BF16)
    for c0 in range(0, CONV_DIM, _N_CHUNK):
        cols = slice(c0, c0 + _N_CHUNK)
        main = main_ref[0, :, cols]
        ext = jnp.concatenate([prev_ref[0, :, cols] * prev_ok, main, next_ref[0, :, cols] * next_ok],
                              axis=0)
        conv = (_dot(sp_ref[...], ext) * cw_ref[0:1, cols] + main.astype(F32) * cw_ref[1:2, cols]
                + _dot(sn_ref[...], ext) * cw_ref[2:3, cols] + cb_ref[:, cols])
        o_ref[0, :, cols] = _silu(conv).astype(o_ref.dtype)


def _xbc_conv(xbc, shift_prev, shift_next, conv_w, conv_b):
    b, s, _ = xbc.shape
    nc = s // CHUNK
    nhalo = s // HALO
    per = CHUNK // HALO
    consts = (shift_prev, shift_next, conv_w, conv_b)
    return pl.pallas_call(
        functools.partial(_xbc_conv_kernel, nc=nc),
        grid=(b, nc),
        in_specs=[pl.BlockSpec((1, CHUNK, CONV_DIM), lambda i, c: (i, c, 0)),
                  pl.BlockSpec((1, HALO, CONV_DIM), lambda i, c: (i, jnp.maximum(c * per - 1, 0), 0)),
                  pl.BlockSpec((1, HALO, CONV_DIM),
                               lambda i, c: (i, jnp.minimum((c + 1) * per, nhalo - 1), 0))]
                 + [_const_spec(c.shape) for c in consts],
        out_specs=pl.BlockSpec((1, CHUNK, CONV_DIM), lambda i, c: (i, c, 0)),
        out_shape=jax.ShapeDtypeStruct((b, s, CONV_DIM), BF16),
        compiler_params=_cparams("parallel", "parallel"),
        name="xbc_conv",
    )(xbc, xbc, xbc, *consts)


def _ssd_kernel(*refs, reverse):
    if reverse:
        (xc_ref, dt_ref, z_ref, yf_ref, dtb_ref, a_ref, tri_ref, e_ref, dskip_ref, nw_ref,
         o_ref, h_ref) = refs
    else:
        xc_ref, dt_ref, dtb_ref, a_ref, tri_ref, e_ref, o_ref, h_ref = refs
    @pl.when(pl.program_id(1) == 0)
    def _():
        h_ref[...] = jnp.zeros(h_ref.shape, F32)

    for r in range(xc_ref.shape[0]):
        _ssd_chunk(r, refs, reverse)


def _ssd_chunk(r, refs, reverse):
    if reverse:
        (xc_ref, dt_ref, z_ref, yf_ref, dtb_ref, a_ref, tri_ref, e_ref, dskip_ref, nw_ref,
         o_ref, h_ref) = refs
    else:
        xc_ref, dt_ref, dtb_ref, a_ref, tri_ref, e_ref, o_ref, h_ref = refs
    L = CHUNK
    nh = N_SSM_HEADS
    x = xc_ref[r, :, 0:D_INNER].astype(F32)

    dt_raw = dt_ref[r] + dtb_ref[...]
    dtv = jnp.maximum(dt_raw, 0.0) + jnp.log(1.0 + jnp.exp(-jnp.abs(dt_raw)))
    da = dtv * a_ref[...]

    tri = tri_ref[...]
    da_sel = _split_select(da)
    part = _dot(tri, da_sel)
    cum = part
    for cpy in range(1, HEAD_LANE_COPIES):
        cum = cum + pltpu.roll(part, cpy * nh, 1)
    part_t = lax.dot_general(da_sel, tri, (((0,), (1,)), ((), ())), preferred_element_type=F32)
    cum_t = part_t[:nh] + part_t[nh:2 * nh] + part_t[2 * nh:3 * nh]
    last = 0 if reverse else L - 1
    tot = cum[last:last + 1, :]

    dt_e = _expand_heads(dtv, e_ref)
    grow_e = _expand_heads(jnp.exp(cum), e_ref)
    tail_e = _expand_heads(dtv * jnp.exp(tot - cum), e_ref)
    xdt = (x * dt_e).astype(BF16)
    xtail = (x * tail_e).astype(BF16)
    chunk_decay = grow_e[last:last + 1, :]

    li = lax.broadcasted_iota(jnp.int32, (L, L), 0)
    si = lax.broadcasted_iota(jnp.int32, (L, L), 1)
    mask = (si >= li) if reverse else (si <= li)
    lane = lax.broadcasted_iota(jnp.int32, (L, 2 * SSM_HEADDIM), 1)
    first_head = lane < SSM_HEADDIM

    y_parts = []
    for g in range(N_SSM_GROUPS):
        b0 = D_INNER + g * D_STATE
        c0 = D_INNER + (N_SSM_GROUPS + g) * D_STATE
        b_bf = xc_ref[r, :, b0:b0 + D_STATE]
        c_bf = xc_ref[r, :, c0:c0 + D_STATE]
        cb = _dot_nt(c_bf, b_bf)
        gs = slice(g * GROUP_INNER, (g + 1) * GROUP_INNER)
        h_in = h_ref[r, g]
        y_off = _dot(c_bf, h_in.astype(BF16)) * grow_e[:, gs]
        b_t = b_bf.astype(F32).T.astype(BF16)
        h_ref[r, g] = h_in * chunk_decay[:, gs] + _dot(b_t, xtail[:, gs])
        diag = []
        for pr in range(HEADS_PER_GROUP // 2):
            ms = []
            for hh in (g * HEADS_PER_GROUP + 2 * pr, g * HEADS_PER_GROUP + 2 * pr + 1):
                seg = cum[:, hh:hh + 1] - cum_t[hh:hh + 1, :]
                dec = jnp.exp(jnp.where(mask, seg, MASKED_SEG))
                ms.append((cb * dec).astype(BF16))
            lhs = jnp.concatenate(ms, axis=1)
            xp = xdt[:, g * GROUP_INNER + pr * 128:g * GROUP_INNER + (pr + 1) * 128]
            zero = jnp.zeros_like(xp)
            rhs = jnp.concatenate([jnp.where(first_head, xp, zero),
                                   jnp.where(first_head, zero, xp)], axis=0)
            diag.append(_dot(lhs, rhs))
        y_parts.append(jnp.concatenate(diag, axis=1) + y_off)
    y = jnp.concatenate(y_parts, axis=1)

    if not reverse:
        o_ref[r] = y.astype(o_ref.dtype)
    else:
        y = y + yf_ref[r].astype(F32) + x * dskip_ref[...]
        y = y * _silu(z_ref[r].astype(F32))
        outs = []
        for g in range(N_SSM_GROUPS):
            yg = y[:, g * GROUP_INNER:(g + 1) * GROUP_INNER]
            outs.append(yg * lax.rsqrt(jnp.mean(yg * yg, axis=-1, keepdims=True) + NORM_EPS))
        o_ref[r] = (jnp.concatenate(outs, axis=1) * nw_ref[...]).astype(o_ref.dtype)


def _ssd_pass(xc, dt, consts, reverse, extra=()):
    b, s, _ = xc.shape
    nc = s // CHUNK
    rows = math.gcd(b, SSD_BATCH_ROWS)
    chunk_map = (lambda i, c: (i, nc - 1 - c, 0)) if reverse else (lambda i, c: (i, c, 0))
    dt_map = (lambda i, c: (i, nc - 1 - c, 1)) if reverse else (lambda i, c: (i, c, 0))
    in_specs = [pl.BlockSpec((rows, CHUNK, CONV_DIM), chunk_map),
                pl.BlockSpec((rows, CHUNK, V7X_LANES), dt_map)]
    args = [xc, dt]
    for e in extra:
        in_specs.append(pl.BlockSpec((rows, CHUNK, D_INNER), chunk_map))
        args.append(e)
    for cst in consts:
        in_specs.append(_const_spec(cst.shape))
        args.append(cst)
    return pl.pallas_call(
        functools.partial(_ssd_kernel, reverse=reverse),
        grid=(b // rows, nc),
        in_specs=in_specs,
        out_specs=pl.BlockSpec((rows, CHUNK, D_INNER), chunk_map),
        out_shape=jax.ShapeDtypeStruct((b, s, D_INNER), BF16),
        scratch_shapes=[pltpu.VMEM((rows, N_SSM_GROUPS, D_STATE, GROUP_INNER), F32)],
        compiler_params=_cparams("parallel", "arbitrary"),
        name="ssd_bwd" if reverse else "ssd_fwd",
    )(*args)


def _merge_kernel(x_ref, attn_ref, ssd_ref, gate_ref, gb_ref, wa_ref, ws_ref, wo_ref, nw_ref,
                  x1_ref, h2_ref):
    g = _sigmoid(gate_ref[...].astype(F32) + gb_ref[...])
    merged = (g[:, :D_MODEL] * _dot(attn_ref[...], wa_ref[...])
              + g[:, D_MODEL:] * _dot(ssd_ref[...], ws_ref[...]))
    x1 = x_ref[...] + _dot(merged.astype(BF16), wo_ref[...])
    x1_ref[...] = x1
    h2_ref[...] = _rms_rows(x1, nw_ref[...]).astype(BF16)


def _merge(x2d, attn, ssd, gates, gate_b, wa, ws, wo, norm2_w):
    t = x2d.shape[0]
    tm = min(ROW_TILE, t)
    row = lambda w: pl.BlockSpec((tm, w), lambda i: (i, 0))
    return pl.pallas_call(
        _merge_kernel,
        grid=(t // tm,),
        in_specs=[row(D_MODEL), row(ATTN_DIM), row(D_INNER), row(2 * D_MODEL),
                  _const_spec(gate_b.shape), _const_spec(wa.shape), _const_spec(ws.shape),
                  _const_spec(wo.shape), _const_spec(norm2_w.shape)],
        out_specs=[row(D_MODEL), row(D_MODEL)],
        out_shape=[jax.ShapeDtypeStruct((t, D_MODEL), F32),
                   jax.ShapeDtypeStruct((t, D_MODEL), BF16)],
        compiler_params=_cparams("parallel"),
        name="merge",
    )(x2d, attn, ssd, gates, gate_b, wa, ws, wo, norm2_w)


def _ffn_up_kernel(h_ref, w_ref, u_ref):
    h = h_ref[...]
    n = w_ref.shape[1]
    for c in range(0, n, _N_CHUNK):
        u_ref[:, c:c + _N_CHUNK] = _dot(h, w_ref[:, c:c + _N_CHUNK]).astype(u_ref.dtype)


def _ffn_up(h2, w_up):
    t = h2.shape[0]
    tm = min(ROW_TILE, t)
    n = w_up.shape[1]
    return pl.pallas_call(
        _ffn_up_kernel,
        grid=(t // tm,),
        in_specs=[pl.BlockSpec((tm, D_MODEL), lambda i: (i, 0)), _const_spec(w_up.shape)],
        out_specs=pl.BlockSpec((tm, n), lambda i: (i, 0)),
        out_shape=jax.ShapeDtypeStruct((t, n), BF16),
        compiler_params=_cparams("parallel"),
        name="ffn_up",
    )(h2, w_up)


_FFN_COL_CHUNK = 256


def _gelu_tanh(x):
    return 0.5 * x * (1.0 + jnp.tanh(math.sqrt(2.0 / math.pi) * (x + 0.044715 * (x * x * x))))


def _ffn_down_kernel(u_ref, prev_ref, next_ref, x1_ref, p_ref, sp_ref, sn_ref, cw_ref, cb_ref,
                     wd_ref, wple_ref, wpg_ref, bpg_ref, fw_ref, o_ref, act_ref, *, tm, tiles_per_seq):
    j = pl.program_id(0) % tiles_per_seq
    prev_ok = (j > 0).astype(BF16)
    next_ok = (j < tiles_per_seq - 1).astype(BF16)
    sp = sp_ref[...]
    sn = sn_ref[...]

    def conv(r0, cols):
        main = u_ref[r0:r0 + CHUNK, cols]
        before = prev_ref[:, cols] * prev_ok if r0 == 0 else u_ref[r0 - HALO:r0, cols]
        last = r0 + CHUNK == tm
        after = next_ref[:, cols] * next_ok if last else u_ref[r0 + CHUNK:r0 + CHUNK + HALO, cols]
        ext = jnp.concatenate([before, main, after], axis=0)
        return (_dot(sp, ext) * cw_ref[0:1, cols] + main.astype(F32) * cw_ref[1:2, cols]
                + _dot(sn, ext) * cw_ref[2:3, cols] + cb_ref[:, cols])

    for r0 in range(0, tm, CHUNK):
        for c0 in range(0, D_FF, _FFN_COL_CHUNK):
            gate = conv(r0, slice(c0, c0 + _FFN_COL_CHUNK))
            val = conv(r0, slice(D_FF + c0, D_FF + c0 + _FFN_COL_CHUNK))
            act_ref[r0:r0 + CHUNK, c0:c0 + _FFN_COL_CHUNK] = (_gelu_tanh(gate) * val).astype(BF16)

    x2 = x1_ref[...] + _dot(act_ref[...], wd_ref[...])
    pg = _sigmoid(_dot(x2.astype(BF16), wpg_ref[...]) + bpg_ref[...])
    x3 = x2 + pg * _dot(p_ref[...].astype(BF16), wple_ref[...])
    o_ref[...] = _rms_rows(x3, fw_ref[...])


def _ffn_down(u, x1, p2d, seq, shift_prev, shift_next, conv_w, conv_b, w_down, w_ple, w_pg, b_pg,
              final_w):
    t = u.shape[0]
    tm = min(FFN_ROW_TILE, seq)
    tiles_per_seq = seq // tm
    per = tm // HALO
    nhalo = t // HALO
    n = u.shape[1]
    kern = functools.partial(_ffn_down_kernel, tm=tm, tiles_per_seq=tiles_per_seq)
    consts = (shift_prev, shift_next, conv_w, conv_b, w_down, w_ple, w_pg, b_pg, final_w)
    return pl.pallas_call(
        kern,
        grid=(t // tm,),
        in_specs=[pl.BlockSpec((tm, n), lambda i: (i, 0)),
                  pl.BlockSpec((HALO, n), lambda i: (jnp.maximum(i * per - 1, 0), 0)),
                  pl.BlockSpec((HALO, n), lambda i: (jnp.minimum((i + 1) * per, nhalo - 1), 0)),
                  pl.BlockSpec((tm, D_MODEL), lambda i: (i, 0)),
                  pl.BlockSpec((tm, PLE_DIM), lambda i: (i, 0))]
                 + [_const_spec(c.shape) for c in consts],
        out_specs=pl.BlockSpec((tm, D_MODEL), lambda i: (i, 0)),
        out_shape=jax.ShapeDtypeStruct((t, D_MODEL), F32),
        scratch_shapes=[pltpu.VMEM((tm, D_FF), BF16)],
        compiler_params=_cparams("parallel"),
        name="ffn_down",
    )(u, u, u, x1, p2d, *consts)


def _rope_tables(n_tokens):
    rows = n_tokens // GRID_W
    row_idx = jnp.repeat(jnp.arange(rows, dtype=F32), GRID_W)
    col_idx = jnp.tile(jnp.arange(GRID_W, dtype=F32), rows)
    inv_freq = ROPE_THETA ** (-jnp.arange(0, AXIS_ROT_DIM, 2, dtype=F32) / AXIS_ROT_DIM)
    ang = jnp.concatenate([row_idx[:, None] * inv_freq, col_idx[:, None] * inv_freq], axis=-1)
    cos, sin = jnp.cos(ang), jnp.sin(ang)
    cos_h = jnp.concatenate([cos, cos], axis=-1)
    sin_h = jnp.concatenate([-sin, sin], axis=-1)
    return jnp.tile(cos_h, (1, N_KV_HEADS)), jnp.tile(sin_h, (1, N_KV_HEADS))


def _prepare(norm1_w, w_in, ssm_conv_w, ssm_conv_b, dt_bias, a_log, d_skip, ssd_norm_w,
             q_norm_w, k_norm_w, gate_b, w_attn_branch, w_ssd_branch, w_out, norm2_w, w_up,
             ffn_conv_w, ffn_conv_b, w_down, w_ple, w_ple_gate, b_ple_gate, final_norm_w):
    perm = jnp.concatenate([jnp.arange(0, HEAD_DIM, 2), jnp.arange(1, HEAD_DIM, 2)])
    w = w_in[0]
    sizes = (ATTN_DIM, KV_DIM, KV_DIM, D_INNER, CONV_DIM, 2 * N_SSM_HEADS, 2 * D_MODEL)
    parts, start = [], 0
    for sz in sizes:
        parts.append(w[:, start:start + sz])
        start += sz
    wq, wk, wv, wz, wxbc, wdt, wg = parts
    wq = wq.reshape(D_MODEL, N_HEADS, HEAD_DIM)[:, :, perm].reshape(D_MODEL, ATTN_DIM)
    wk = wk.reshape(D_MODEL, N_KV_HEADS, HEAD_DIM)[:, :, perm].reshape(D_MODEL, KV_DIM)
    wdt = jnp.concatenate([wdt[:, :N_SSM_HEADS]] * HEAD_LANE_COPIES
                          + [wdt[:, N_SSM_HEADS:]] * HEAD_LANE_COPIES, axis=1)
    w_all = jnp.concatenate([wq, wk, wv, wz, wxbc, wg, wdt], axis=1).astype(BF16)

    head_id = jnp.arange(KV_DIM) // HEAD_DIM
    ones_blk = (head_id[:, None] == head_id[None, :]).astype(BF16)
    li = jnp.arange(CHUNK)
    tri_fwd = (li[None, :] <= li[:, None]).astype(BF16)
    tri_bwd = (li[None, :] >= li[:, None]).astype(BF16)
    chan_head = jnp.arange(D_INNER) // SSM_HEADDIM
    e1 = (jnp.arange(N_SSM_HEADS)[:, None] == chan_head[None, :]).astype(BF16)
    e_rows = jnp.concatenate([e1] * (HEAD_LANE_COPIES - 1) + [jnp.zeros_like(e1)], axis=0)
    ext = jnp.arange(CHUNK + 2 * HALO)
    shift_prev = (ext[None, :] == li[:, None] + HALO - 1).astype(BF16)
    shift_next = (ext[None, :] == li[:, None] + HALO + 1).astype(BF16)
    a = jnp.tile(-jnp.exp(a_log[0].astype(F32)), (1, HEAD_LANE_COPIES))
    dtb = jnp.tile(dt_bias[0], (1, HEAD_LANE_COPIES))
    return dict(
        norm1_w=norm1_w[0][None, :], w_all=w_all,
        ones_blk=ones_blk,
        qw=jnp.tile(q_norm_w[0][perm], GQA_GROUP)[None, :],
        kw=jnp.tile(k_norm_w[0][perm], N_KV_HEADS)[None, :],
        conv_w=ssm_conv_w[0], conv_b=ssm_conv_b[0][None, :],
        dtb_f=dtb[0][None, :], dtb_b=dtb[1][None, :],
        a_f=a[0][None, :], a_b=a[1][None, :],
        tri_f=tri_fwd, tri_b=tri_bwd, e_rows=e_rows,
        shift_prev=shift_prev, shift_next=shift_next,
        dskip=jnp.repeat(d_skip[0], SSM_HEADDIM)[None, :], ssd_nw=ssd_norm_w[0][None, :],
        gate_b=gate_b[0][None, :], wa=w_attn_branch[0].astype(BF16),
        ws=w_ssd_branch[0].astype(BF16), wo=w_out[0].astype(BF16), norm2_w=norm2_w[0][None, :],
        w_up=w_up[0].astype(BF16), ffn_cw=ffn_conv_w[0], ffn_cb=ffn_conv_b[0][None, :],
        w_down=w_down[0].astype(BF16), w_ple=w_ple[0].astype(BF16),
        w_pg=w_ple_gate[0].astype(BF16), b_pg=b_ple_gate[0][None, :],
        final_w=final_norm_w[None, :])


def _trunk(x, p, prm):
    b, s, _ = x.shape
    t = b * s
    x2d = x.reshape(t, D_MODEL)
    q, k, v, z, xbc, gates, dt = _in_proj(x2d, prm["norm1_w"], prm["w_all"])
    cos, sin_signed = _rope_tables(s)

    ts = min(KV_PREP_TILE, s)
    k_heads, vt_heads = _kv_prep(k.reshape(b, s, KV_DIM), v.reshape(b, s, KV_DIM), cos, sin_signed,
                                 prm["ones_blk"], prm["kw"], ts)
    attn = _attention(q.reshape(b, s, ATTN_DIM), k_heads, vt_heads, cos, sin_signed,
                      prm["ones_blk"], prm["qw"], min(ATTN_Q_TILE, s))

    xc = _xbc_conv(xbc.reshape(b, s, CONV_DIM), prm["shift_prev"], prm["shift_next"],
                   prm["conv_w"], prm["conv_b"])
    dt3 = dt.reshape(b, s, 2 * V7X_LANES)
    y_fwd = _ssd_pass(xc, dt3, (prm["dtb_f"], prm["a_f"], prm["tri_f"], prm["e_rows"]),
                      reverse=False)
    ssd = _ssd_pass(xc, dt3, (prm["dtb_b"], prm["a_b"], prm["tri_b"], prm["e_rows"],
                              prm["dskip"], prm["ssd_nw"]),
                    reverse=True, extra=(z.reshape(b, s, D_INNER), y_fwd))

    x1, h2 = _merge(x2d, attn.reshape(t, ATTN_DIM), ssd.reshape(t, D_INNER), gates,
                    prm["gate_b"], prm["wa"], prm["ws"], prm["wo"], prm["norm2_w"])
    u = _ffn_up(h2, prm["w_up"])
    out = _ffn_down(u, x1, p[0].reshape(t, PLE_DIM), s, prm["shift_prev"], prm["shift_next"],
                    prm["ffn_cw"], prm["ffn_cb"], prm["w_down"], prm["w_ple"], prm["w_pg"],
                    prm["b_pg"], prm["final_w"])
    return out.reshape(b, s, D_MODEL)


def kernel(x_prompt, x_sample, p_prompt, p_sample, norm1_w, w_in, ssm_conv_w, ssm_conv_b, dt_bias,
           a_log, d_skip, ssd_norm_w, q_norm_w, k_norm_w, gate_b, w_attn_branch, w_ssd_branch,
           w_out, norm2_w, w_up, ffn_conv_w, ffn_conv_b, w_down, w_ple, w_ple_gate, b_ple_gate,
           final_norm_w):
    prm = _prepare(norm1_w, w_in, ssm_conv_w, ssm_conv_b, dt_bias, a_log, d_skip, ssd_norm_w,
                   q_norm_w, k_norm_w, gate_b, w_attn_branch, w_ssd_branch, w_out, norm2_w, w_up,
                   ffn_conv_w, ffn_conv_b, w_down, w_ple, w_ple_gate, b_ple_gate, final_norm_w)
    return (_trunk(x_prompt, p_prompt, prm), _trunk(x_sample, p_sample, prm))
```

```python
import functools
import math

import jax
import jax.numpy as jnp
from jax import lax
from jax.experimental import pallas as pl
from jax.experimental.pallas import tpu as pltpu

F32 = jnp.float32
BF16 = jnp.bfloat16

D_MODEL = 1024
N_HEADS = 16
N_KV_HEADS = 4
HEAD_DIM = 64
GQA_GROUP = N_HEADS // N_KV_HEADS
ATTN_DIM = N_HEADS * HEAD_DIM
KV_DIM = N_KV_HEADS * HEAD_DIM
GROUP_Q_DIM = GQA_GROUP * HEAD_DIM
VT_ROWS = HEAD_DIM + 16
AXIS_ROT_DIM = HEAD_DIM // 2
ROPE_THETA = 10000.0
GRID_W = 64

D_INNER = 2048
SSM_HEADDIM = 64
N_SSM_HEADS = D_INNER // SSM_HEADDIM
N_SSM_GROUPS = 4
HEADS_PER_GROUP = N_SSM_HEADS // N_SSM_GROUPS
D_STATE = 128
GROUP_INNER = HEADS_PER_GROUP * SSM_HEADDIM
CONV_DIM = D_INNER + 2 * N_SSM_GROUPS * D_STATE
CHUNK = 128

D_FF = 2816
PLE_DIM = 256
NORM_EPS = 1e-6

V7X_LANES = 128
V7X_BF16_SUBLANES = 16
V7X_VMEM_LIMIT_BYTES = 60000 * 1024

MASKED_SEG = -1e30
LOG2E = 1.4426950408889634

ROW_TILE = 512
KV_PREP_TILE = 4096
ATTN_Q_TILE = 512
ATTN_KEY_SUB = 256
ATTN_Q_SLAB = 512
ATTN_PIPE_DEPTH = 4
FFN_ROW_TILE = 512
SSD_BATCH_ROWS = 2
HALO = V7X_BF16_SUBLANES


def _cparams(*sem):
    return pltpu.CompilerParams(dimension_semantics=sem, vmem_limit_bytes=V7X_VMEM_LIMIT_BYTES)


def _const_spec(shape):
    nd = len(shape)
    return pl.BlockSpec(shape, lambda *_: (0,) * nd, pipeline_mode=pl.Buffered(1))


def _dot(a, b):
    return jnp.dot(a, b, preferred_element_type=F32)


def _dot_nt(a, b):
    return lax.dot_general(a, b, (((1,), (1,)), ((), ())), preferred_element_type=F32)


def _rms_rows(x, w):
    return x * lax.rsqrt(jnp.mean(x * x, axis=-1, keepdims=True) + NORM_EPS) * w


def _sigmoid(x):
    return 1.0 / (1.0 + jnp.exp(-x))


_IN_PROJ_OUTS = (
    (ATTN_DIM, BF16), (KV_DIM, BF16), (KV_DIM, BF16), (D_INNER, BF16), (CONV_DIM, BF16),
    (2 * D_MODEL, BF16), (2 * V7X_LANES, F32))
_N_CHUNK = 512


def _in_proj_kernel(x_ref, nw_ref, w_ref, *out_refs):
    h = _rms_rows(x_ref[...], nw_ref[...]).astype(BF16)
    col = 0
    for o_ref, (width, _) in zip(out_refs, _IN_PROJ_OUTS):
        for c in range(0, width, _N_CHUNK):
            cw = min(_N_CHUNK, width - c)
            o_ref[:, c:c + cw] = _dot(h, w_ref[:, col + c:col + c + cw]).astype(o_ref.dtype)
        col += width


def _in_proj(x2d, norm_w, w_all):
    t = x2d.shape[0]
    tm = min(ROW_TILE, t)
    n_all = w_all.shape[1]
    return pl.pallas_call(
        _in_proj_kernel,
        grid=(t // tm,),
        in_specs=[pl.BlockSpec((tm, D_MODEL), lambda i: (i, 0)),
                  _const_spec((1, D_MODEL)),
                  _const_spec((D_MODEL, n_all))],
        out_specs=[pl.BlockSpec((tm, w), lambda i: (i, 0)) for w, _ in _IN_PROJ_OUTS],
        out_shape=[jax.ShapeDtypeStruct((t, w), dt) for w, dt in _IN_PROJ_OUTS],
        compiler_params=_cparams("parallel"),
        name="in_proj",
    )(x2d, norm_w, w_all)


def _head_norm_rope(x, ones_blk, w, cos, sin_signed):
    width = x.shape[-1]
    ss = _dot((x * x).astype(BF16), ones_blk)
    xn = x * lax.rsqrt(ss * (1.0 / HEAD_DIM) + NORM_EPS) * w
    lane = lax.broadcasted_iota(jnp.int32, xn.shape, 1)
    half = HEAD_DIM // 2
    partner = jnp.where((lane % HEAD_DIM) < half,
                        pltpu.roll(xn, width - half, 1), pltpu.roll(xn, half, 1))
    return xn * cos + partner * sin_signed


def _kv_prep_kernel(k_ref, v_ref, cos_ref, sin_ref, ones_ref, kw_ref, ko_ref, vto_ref):
    k = _head_norm_rope(k_ref[0].astype(F32), ones_ref[...], kw_ref[...],
                        cos_ref[...], sin_ref[...])
    vt = v_ref[0].astype(F32).T
    ts = vt.shape[1]
    sub = lax.broadcasted_iota(jnp.int32, (VT_ROWS - HEAD_DIM, ts), 0)
    tail = jnp.where(sub == 0, 1.0, 0.0).astype(BF16)
    for h in range(N_KV_HEADS):
        ko_ref[0, h] = k[:, h * HEAD_DIM:(h + 1) * HEAD_DIM].astype(BF16)
        vto_ref[0, h, 0, 0:HEAD_DIM, :] = vt[h * HEAD_DIM:(h + 1) * HEAD_DIM, :].astype(BF16)
        vto_ref[0, h, 0, HEAD_DIM:VT_ROWS, :] = tail


def _kv_prep(k, v, cos, sin_signed, ones_blk, kw, ts):
    b, s, _ = k.shape
    nk = s // ts
    return pl.pallas_call(
        _kv_prep_kernel,
        grid=(b, nk),
        in_specs=[pl.BlockSpec((1, ts, KV_DIM), lambda i, j: (i, j, 0)),
                  pl.BlockSpec((1, ts, KV_DIM), lambda i, j: (i, j, 0)),
                  pl.BlockSpec((ts, KV_DIM), lambda i, j: (j, 0)),
                  pl.BlockSpec((ts, KV_DIM), lambda i, j: (j, 0)),
                  _const_spec((KV_DIM, KV_DIM)),
                  _const_spec((1, KV_DIM))],
        out_specs=[pl.BlockSpec((1, N_KV_HEADS, ts, HEAD_DIM), lambda i, j: (i, 0, j, 0)),
                   pl.BlockSpec((1, N_KV_HEADS, 1, VT_ROWS, ts), lambda i, j: (i, 0, j, 0, 0))],
        out_shape=[jax.ShapeDtypeStruct((b, N_KV_HEADS, s, HEAD_DIM), BF16),
                   jax.ShapeDtypeStruct((b, N_KV_HEADS, nk, VT_ROWS, ts), BF16)],
        compiler_params=_cparams("parallel", "parallel"),
        name="kv_prep",
    )(k, v, cos, sin_signed, ones_blk, kw)


def _attn_kernel(q_ref, k_ref, vt_ref, cos_ref, sin_ref, ones_ref, qw_ref, o_ref,
                 qs_ref, m_ref, acc_ref, sp_ref, *, tq, tk, nk):
    q = _head_norm_rope(q_ref[0].astype(F32), ones_ref[...], qw_ref[...],
                        cos_ref[...], sin_ref[...])
    q_t = (q * (HEAD_DIM ** -0.5 * LOG2E)).T
    for h in range(GQA_GROUP):
        qs_ref[:, h * tq:(h + 1) * tq] = q_t[h * HEAD_DIM:(h + 1) * HEAD_DIM, :].astype(BF16)
    m_ref[...] = jnp.full(m_ref.shape, -jnp.inf, F32)
    acc_ref[...] = jnp.zeros(acc_ref.shape, F32)

    units = [(i, n) for i in range(tk // ATTN_KEY_SUB)
             for n in range(GQA_GROUP * tq // ATTN_Q_SLAB)]
    n_units = len(units)
    depth = min(ATTN_PIPE_DEPTH, n_units)

    def scores(j, unit):
        i, n = unit
        row0 = pl.multiple_of(j * tk + i * ATTN_KEY_SUB, ATTN_KEY_SUB)
        k = k_ref[0, 0, pl.ds(row0, ATTN_KEY_SUB), :]
        return _dot(k, qs_ref[:, n * ATTN_Q_SLAB:(n + 1) * ATTN_Q_SLAB])

    def softmax_pv(j, unit, s_t):
        i, n = unit
        cs = slice(n * ATTN_Q_SLAB, (n + 1) * ATTN_Q_SLAB)
        v_i = vt_ref[0, 0, j, :, i * ATTN_KEY_SUB:(i + 1) * ATTN_KEY_SUB]
        m_prev = m_ref[:, cs]
        s_b = s_t.astype(BF16)
        m_new = jnp.maximum(m_prev, jnp.max(s_b, axis=0, keepdims=True).astype(F32))
        p = jnp.exp2(s_b - m_new.astype(BF16))
        alpha = jnp.exp2(m_prev - m_new)
        acc_ref[:, cs] = alpha * acc_ref[:, cs] + _dot(v_i, p)
        m_ref[:, cs] = m_new

    for d in range(depth):
        sp_ref[d] = scores(0, units[d])

    def kv_step(j, carry):
        j_next = jnp.minimum(j + 1, nk - 1)
        pending = {}
        for idx, u in enumerate(units):
            ahead = idx + depth
            if ahead < n_units:
                pending[ahead] = scores(j, units[ahead])
            s_t = sp_ref[idx] if idx < depth else pending.pop(idx)
            softmax_pv(j, u, s_t)
            if ahead >= n_units:
                sp_ref[ahead - n_units] = scores(j_next, units[ahead - n_units])
        return carry

    lax.fori_loop(0, nk, kv_step, 0)
    out_t = acc_ref[0:HEAD_DIM, :] * (1.0 / acc_ref[HEAD_DIM:HEAD_DIM + 1, :])
    out_t = jnp.concatenate([out_t[:, h * tq:(h + 1) * tq] for h in range(GQA_GROUP)], axis=0)
    o_ref[0] = out_t.T.astype(o_ref.dtype)


def _attention(q, k_heads, vt_heads, cos, sin_signed, ones_blk, qw, tq):
    b, s, _ = q.shape
    nk, tk = vt_heads.shape[2], vt_heads.shape[4]
    vq = GQA_GROUP * tq
    kern = functools.partial(_attn_kernel, tq=tq, tk=tk, nk=nk)
    return pl.pallas_call(
        kern,
        grid=(b, N_KV_HEADS, s // tq),
        in_specs=[pl.BlockSpec((1, tq, GROUP_Q_DIM), lambda i, g, j: (i, j, g)),
                  pl.BlockSpec((1, 1, s, HEAD_DIM), lambda i, g, j: (i, g, 0, 0)),
                  pl.BlockSpec((1, 1, nk, VT_ROWS, tk), lambda i, g, j: (i, g, 0, 0, 0)),
                  pl.BlockSpec((tq, GROUP_Q_DIM), lambda i, g, j: (j, 0)),
                  pl.BlockSpec((tq, GROUP_Q_DIM), lambda i, g, j: (j, 0)),
                  _const_spec((GROUP_Q_DIM, GROUP_Q_DIM)),
                  _const_spec((1, GROUP_Q_DIM))],
        out_specs=pl.BlockSpec((1, tq, GROUP_Q_DIM), lambda i, g, j: (i, j, g)),
        out_shape=jax.ShapeDtypeStruct((b, s, ATTN_DIM), BF16),
        scratch_shapes=[pltpu.VMEM((HEAD_DIM, vq), BF16),
                        pltpu.VMEM((1, vq), F32),
                        pltpu.VMEM((VT_ROWS, vq), F32),
                        pltpu.VMEM((ATTN_PIPE_DEPTH, ATTN_KEY_SUB, ATTN_Q_SLAB), F32)],
        compiler_params=_cparams("parallel", "parallel", "arbitrary"),
        name="attention",
    )(q, k_heads, vt_heads, cos, sin_signed, ones_blk, qw)


HEAD_LANE_COPIES = V7X_LANES // N_SSM_HEADS


def _split_select(v):
    hi = v.astype(BF16)
    r1 = v - hi.astype(F32)
    mid = r1.astype(BF16)
    lo = (r1 - mid.astype(F32)).astype(BF16)
    copy = lax.broadcasted_iota(jnp.int32, v.shape, 1) // N_SSM_HEADS
    return jnp.where(copy == 0, hi, jnp.where(copy == 1, mid, jnp.where(copy == 2, lo, jnp.zeros_like(lo))))


def _expand_heads(v, e_ref):
    return _dot(_split_select(v), e_ref[...])


def _silu(x):
    return x * (1.0 / (1.0 + jnp.exp2(x * (-LOG2E))))


def _xbc_conv_kernel(main_ref, prev_ref, next_ref, sp_ref, sn_ref, cw_ref, cb_ref, o_ref, *, nc):
    cc = pl.program_id(1)
    prev_ok = (cc > 0).astype(BF16)
    next_ok = (cc < nc - 1).astype(BF16)
    for c0 in range(0, CONV_DIM, _N_CHUNK):
        cols = slice(c0, c0 + _N_CHUNK)
        main = main_ref[0, :, cols]
        ext = jnp.concatenate([prev_ref[0, :, cols] * prev_ok, main, next_ref[0, :, cols] * next_ok],
                              axis=0)
        conv = (_dot(sp_ref[...], ext) * cw_ref[0:1, cols] + main.astype(F32) * cw_ref[1:2, cols]
                + _dot(sn_ref[...], ext) * cw_ref[2:3, cols] + cb_ref[:, cols])
        o_ref[0, :, cols] = _silu(conv).astype(o_ref.dtype)


def _xbc_conv(xbc, shift_prev, shift_next, conv_w, conv_b):
    b, s, _ = xbc.shape
    nc = s // CHUNK
    nhalo = s // HALO
    per = CHUNK // HALO
    consts = (shift_prev, shift_next, conv_w, conv_b)
    return pl.pallas_call(
        functools.partial(_xbc_conv_kernel, nc=nc),
        grid=(b, nc),
        in_specs=[pl.BlockSpec((1, CHUNK, CONV_DIM), lambda i, c: (i, c, 0)),
                  pl.BlockSpec((1, HALO, CONV_DIM), lambda i, c: (i, jnp.maximum(c * per - 1, 0), 0)),
                  pl.BlockSpec((1, HALO, CONV_DIM),
                               lambda i, c: (i, jnp.minimum((c + 1) * per, nhalo - 1), 0))]
                 + [_const_spec(c.shape) for c in consts],
        out_specs=pl.BlockSpec((1, CHUNK, CONV_DIM), lambda i, c: (i, c, 0)),
        out_shape=jax.ShapeDtypeStruct((b, s, CONV_DIM), BF16),
        compiler_params=_cparams("parallel", "parallel"),
        name="xbc_conv",
    )(xbc, xbc, xbc, *consts)


def _ssd_kernel(*refs, reverse):
    if reverse:
        (xc_ref, dt_ref, z_ref, yf_ref, dtb_ref, a_ref, tri_ref, e_ref, dskip_ref, nw_ref,
         o_ref, h_ref) = refs
    else:
        xc_ref, dt_ref, dtb_ref, a_ref, tri_ref, e_ref, o_ref, h_ref = refs
    @pl.when(pl.program_id(1) == 0)
    def _():
        h_ref[...] = jnp.zeros(h_ref.shape, F32)

    for r in range(xc_ref.shape[0]):
        _ssd_chunk(r, refs, reverse)


def _ssd_chunk(r, refs, reverse):
    if reverse:
        (xc_ref, dt_ref, z_ref, yf_ref, dtb_ref, a_ref, tri_ref, e_ref, dskip_ref, nw_ref,
         o_ref, h_ref) = refs
    else:
        xc_ref, dt_ref, dtb_ref, a_ref, tri_ref, e_ref, o_ref, h_ref = refs
    L = CHUNK
    nh = N_SSM_HEADS
    x = xc_ref[r, :, 0:D_INNER].astype(F32)

    dt_raw = dt_ref[r] + dtb_ref[...]
    dtv = jnp.maximum(dt_raw, 0.0) + jnp.log(1.0 + jnp.exp(-jnp.abs(dt_raw)))
    da = dtv * a_ref[...]

    tri = tri_ref[...]
    da_sel = _split_select(da)
    part = _dot(tri, da_sel)
    cum = part
    for cpy in range(1, HEAD_LANE_COPIES):
        cum = cum + pltpu.roll(part, cpy * nh, 1)
    part_t = lax.dot_general(da_sel, tri, (((0,), (1,)), ((), ())), preferred_element_type=F32)
    cum_t = part_t[:nh] + part_t[nh:2 * nh] + part_t[2 * nh:3 * nh]
    last = 0 if reverse else L - 1
    tot = cum[last:last + 1, :]

    dt_e = _expand_heads(dtv, e_ref)
    grow_e = _expand_heads(jnp.exp(cum), e_ref)
    tail_e = _expand_heads(dtv * jnp.exp(tot - cum), e_ref)
    xdt = (x * dt_e).astype(BF16)
    xtail = (x * tail_e).astype(BF16)
    chunk_decay = grow_e[last:last + 1, :]

    li = lax.broadcasted_iota(jnp.int32, (L, L), 0)
    si = lax.broadcasted_iota(jnp.int32, (L, L), 1)
    mask = (si >= li) if reverse else (si <= li)
    lane = lax.broadcasted_iota(jnp.int32, (L, 2 * SSM_HEADDIM), 1)
    first_head = lane < SSM_HEADDIM

    y_parts = []
    for g in range(N_SSM_GROUPS):
        b0 = D_INNER + g * D_STATE
        c0 = D_INNER + (N_SSM_GROUPS + g) * D_STATE
        b_bf = xc_ref[r, :, b0:b0 + D_STATE]
        c_bf = xc_ref[r, :, c0:c0 + D_STATE]
        cb = _dot_nt(c_bf, b_bf)
        gs = slice(g * GROUP_INNER, (g + 1) * GROUP_INNER)
        h_in = h_ref[r, g]
        y_off = _dot(c_bf, h_in.astype(BF16)) * grow_e[:, gs]
        b_t = b_bf.astype(F32).T.astype(BF16)
        h_ref[r, g] = h_in * chunk_decay[:, gs] + _dot(b_t, xtail[:, gs])
        diag = []
        for pr in range(HEADS_PER_GROUP // 2):
            ms = []
            for hh in (g * HEADS_PER_GROUP + 2 * pr, g * HEADS_PER_GROUP + 2 * pr + 1):
                seg = cum[:, hh:hh + 1] - cum_t[hh:hh + 1, :]
                dec = jnp.exp(jnp.where(mask, seg, MASKED_SEG))
                ms.append((cb * dec).astype(BF16))
            lhs = jnp.concatenate(ms, axis=1)
            xp = xdt[:, g * GROUP_INNER + pr * 128:g * GROUP_INNER + (pr + 1) * 128]
            zero = jnp.zeros_like(xp)
            rhs = jnp.concatenate([jnp.where(first_head, xp, zero),
                                   jnp.where(first_head, zero, xp)], axis=0)
            diag.append(_dot(lhs, rhs))
        y_parts.append(jnp.concatenate(diag, axis=1) + y_off)
    y = jnp.concatenate(y_parts, axis=1)

    if not reverse:
        o_ref[r] = y.astype(o_ref.dtype)
    else:
        y = y + yf_ref[r].astype(F32) + x * dskip_ref[...]
        y = y * _silu(z_ref[r].astype(F32))
        outs = []
        for g in range(N_SSM_GROUPS):
            yg = y[:, g * GROUP_INNER:(g + 1) * GROUP_INNER]
            outs.append(yg * lax.rsqrt(jnp.mean(yg * yg, axis=-1, keepdims=True) + NORM_EPS))
        o_ref[r] = (jnp.concatenate(outs, axis=1) * nw_ref[...]).astype(o_ref.dtype)


def _ssd_pass(xc, dt, consts, reverse, extra=()):
    b, s, _ = xc.shape
    nc = s // CHUNK
    rows = math.gcd(b, SSD_BATCH_ROWS)
    chunk_map = (lambda i, c: (i, nc - 1 - c, 0)) if reverse else (lambda i, c: (i, c, 0))
    dt_map = (lambda i, c: (i, nc - 1 - c, 1)) if reverse else (lambda i, c: (i, c, 0))
    in_specs = [pl.BlockSpec((rows, CHUNK, CONV_DIM), chunk_map),
                pl.BlockSpec((rows, CHUNK, V7X_LANES), dt_map)]
    args = [xc, dt]
    for e in extra:
        in_specs.append(pl.BlockSpec((rows, CHUNK, D_INNER), chunk_map))
        args.append(e)
    for cst in consts:
        in_specs.append(_const_spec(cst.shape))
        args.append(cst)
    return pl.pallas_call(
        functools.partial(_ssd_kernel, reverse=reverse),
        grid=(b // rows, nc),
        in_specs=in_specs,
        out_specs=pl.BlockSpec((rows, CHUNK, D_INNER), chunk_map),
        out_shape=jax.ShapeDtypeStruct((b, s, D_INNER), BF16),
        scratch_shapes=[pltpu.VMEM((rows, N_SSM_GROUPS, D_STATE, GROUP_INNER), F32)],
        compiler_params=_cparams("parallel", "arbitrary"),
        name="ssd_bwd" if reverse else "ssd_fwd",
    )(*args)


def _merge_kernel(x_ref, attn_ref, ssd_ref, gate_ref, gb_ref, wa_ref, ws_ref, wo_ref, nw_ref,
                  x1_ref, h2_ref):
    g = _sigmoid(gate_ref[...].astype(F32) + gb_ref[...])
    merged = (g[:, :D_MODEL] * _dot(attn_ref[...], wa_ref[...])
              + g[:, D_MODEL:] * _dot(ssd_ref[...], ws_ref[...]))
    x1 = x_ref[...] + _dot(merged.astype(BF16), wo_ref[...])
    x1_ref[...] = x1
    h2_ref[...] = _rms_rows(x1, nw_ref[...]).astype(BF16)


def _merge(x2d, attn, ssd, gates, gate_b, wa, ws, wo, norm2_w):
    t = x2d.shape[0]
    tm = min(ROW_TILE, t)
    row = lambda w: pl.BlockSpec((tm, w), lambda i: (i, 0))
    return pl.pallas_call(
        _merge_kernel,
        grid=(t // tm,),
        in_specs=[row(D_MODEL), row(ATTN_DIM), row(D_INNER), row(2 * D_MODEL),
                  _const_spec(gate_b.shape), _const_spec(wa.shape), _const_spec(ws.shape),
                  _const_spec(wo.shape), _const_spec(norm2_w.shape)],
        out_specs=[row(D_MODEL), row(D_MODEL)],
        out_shape=[jax.ShapeDtypeStruct((t, D_MODEL), F32),
                   jax.ShapeDtypeStruct((t, D_MODEL), BF16)],
        compiler_params=_cparams("parallel"),
        name="merge",
    )(x2d, attn, ssd, gates, gate_b, wa, ws, wo, norm2_w)


def _ffn_up_kernel(h_ref, w_ref, u_ref):
    h = h_ref[...]
    n = w_ref.shape[1]
    for c in range(0, n, _N_CHUNK):
        u_ref[:, c:c + _N_CHUNK] = _dot(h, w_ref[:, c:c + _N_CHUNK]).astype(u_ref.dtype)


def _ffn_up(h2, w_up):
    t = h2.shape[0]
    tm = min(ROW_TILE, t)
    n = w_up.shape[1]
    return pl.pallas_call(
        _ffn_up_kernel,
        grid=(t // tm,),
        in_specs=[pl.BlockSpec((tm, D_MODEL), lambda i: (i, 0)), _const_spec(w_up.shape)],
        out_specs=pl.BlockSpec((tm, n), lambda i: (i, 0)),
        out_shape=jax.ShapeDtypeStruct((t, n), BF16),
        compiler_params=_cparams("parallel"),
        name="ffn_up",
    )(h2, w_up)


_FFN_COL_CHUNK = 256


def _gelu_tanh(x):
    return 0.5 * x * (1.0 + jnp.tanh(math.sqrt(2.0 / math.pi) * (x + 0.044715 * (x * x * x))))


def _ffn_down_kernel(u_ref, prev_ref, next_ref, x1_ref, p_ref, sp_ref, sn_ref, cw_ref, cb_ref,
                     wd_ref, wple_ref, wpg_ref, bpg_ref, fw_ref, o_ref, act_ref, *, tm, tiles_per_seq):
    j = pl.program_id(0) % tiles_per_seq
    prev_ok = (j > 0).astype(BF16)
    next_ok = (j < tiles_per_seq - 1).astype(BF16)
    sp = sp_ref[...]
    sn = sn_ref[...]

    def conv(r0, cols):
        main = u_ref[r0:r0 + CHUNK, cols]
        before = prev_ref[:, cols] * prev_ok if r0 == 0 else u_ref[r0 - HALO:r0, cols]
        last = r0 + CHUNK == tm
        after = next_ref[:, cols] * next_ok if last else u_ref[r0 + CHUNK:r0 + CHUNK + HALO, cols]
        ext = jnp.concatenate([before, main, after], axis=0)
        return (_dot(sp, ext) * cw_ref[0:1, cols] + main.astype(F32) * cw_ref[1:2, cols]
                + _dot(sn, ext) * cw_ref[2:3, cols] + cb_ref[:, cols])

    for r0 in range(0, tm, CHUNK):
        for c0 in range(0, D_FF, _FFN_COL_CHUNK):
            gate = conv(r0, slice(c0, c0 + _FFN_COL_CHUNK))
            val = conv(r0, slice(D_FF + c0, D_FF + c0 + _FFN_COL_CHUNK))
            act_ref[r0:r0 + CHUNK, c0:c0 + _FFN_COL_CHUNK] = (_gelu_tanh(gate) * val).astype(BF16)

    x2 = x1_ref[...] + _dot(act_ref[...], wd_ref[...])
    pg = _sigmoid(_dot(x2.astype(BF16), wpg_ref[...]) + bpg_ref[...])
    x3 = x2 + pg * _dot(p_ref[...].astype(BF16), wple_ref[...])
    o_ref[...] = _rms_rows(x3, fw_ref[...])


def _ffn_down(u, x1, p2d, seq, shift_prev, shift_next, conv_w, conv_b, w_down, w_ple, w_pg, b_pg,
              final_w):
    t = u.shape[0]
    tm = min(FFN_ROW_TILE, seq)
    tiles_per_seq = seq // tm
    per = tm // HALO
    nhalo = t // HALO
    n = u.shape[1]
    kern = functools.partial(_ffn_down_kernel, tm=tm, tiles_per_seq=tiles_per_seq)
    consts = (shift_prev, shift_next, conv_w, conv_b, w_down, w_ple, w_pg, b_pg, final_w)
    return pl.pallas_call(
        kern,
        grid=(t // tm,),
        in_specs=[pl.BlockSpec((tm, n), lambda i: (i, 0)),
                  pl.BlockSpec((HALO, n), lambda i: (jnp.maximum(i * per - 1, 0), 0)),
                  pl.BlockSpec((HALO, n), lambda i: (jnp.minimum((i + 1) * per, nhalo - 1), 0)),
                  pl.BlockSpec((tm, D_MODEL), lambda i: (i, 0)),
                  pl.BlockSpec((tm, PLE_DIM), lambda i: (i, 0))]
                 + [_const_spec(c.shape) for c in consts],
        out_specs=pl.BlockSpec((tm, D_MODEL), lambda i: (i, 0)),
        out_shape=jax.ShapeDtypeStruct((t, D_MODEL), F32),
        scratch_shapes=[pltpu.VMEM((tm, D_FF), BF16)],
        compiler_params=_cparams("parallel"),
        name="ffn_down",
    )(u, u, u, x1, p2d, *consts)


def _rope_tables(n_tokens):
    rows = n_tokens // GRID_W
    row_idx = jnp.repeat(jnp.arange(rows, dtype=F32), GRID_W)
    col_idx = jnp.tile(jnp.arange(GRID_W, dtype=F32), rows)
    inv_freq = ROPE_THETA ** (-jnp.arange(0, AXIS_ROT_DIM, 2, dtype=F32) / AXIS_ROT_DIM)
    ang = jnp.concatenate([row_idx[:, None] * inv_freq, col_idx[:, None] * inv_freq], axis=-1)
    cos, sin = jnp.cos(ang), jnp.sin(ang)
    cos_h = jnp.concatenate([cos, cos], axis=-1)
    sin_h = jnp.concatenate([-sin, sin], axis=-1)
    return jnp.tile(cos_h, (1, N_KV_HEADS)), jnp.tile(sin_h, (1, N_KV_HEADS))


def _prepare(norm1_w, w_in, ssm_conv_w, ssm_conv_b, dt_bias, a_log, d_skip, ssd_norm_w,
             q_norm_w, k_norm_w, gate_b, w_attn_branch, w_ssd_branch, w_out, norm2_w, w_up,
             ffn_conv_w, ffn_conv_b, w_down, w_ple, w_ple_gate, b_ple_gate, final_norm_w):
    perm = jnp.concatenate([jnp.arange(0, HEAD_DIM, 2), jnp.arange(1, HEAD_DIM, 2)])
    w = w_in[0]
    sizes = (ATTN_DIM, KV_DIM, KV_DIM, D_INNER, CONV_DIM, 2 * N_SSM_HEADS, 2 * D_MODEL)
    parts, start = [], 0
    for sz in sizes:
        parts.append(w[:, start:start + sz])
        start += sz
    wq, wk, wv, wz, wxbc, wdt, wg = parts
    wq = wq.reshape(D_MODEL, N_HEADS, HEAD_DIM)[:, :, perm].reshape(D_MODEL, ATTN_DIM)
    wk = wk.reshape(D_MODEL, N_KV_HEADS, HEAD_DIM)[:, :, perm].reshape(D_MODEL, KV_DIM)
    wdt = jnp.concatenate([wdt[:, :N_SSM_HEADS]] * HEAD_LANE_COPIES
                          + [wdt[:, N_SSM_HEADS:]] * HEAD_LANE_COPIES, axis=1)
    w_all = jnp.concatenate([wq, wk, wv, wz, wxbc, wg, wdt], axis=1).astype(BF16)

    head_id = jnp.arange(KV_DIM) // HEAD_DIM
    ones_blk = (head_id[:, None] == head_id[None, :]).astype(BF16)
    li = jnp.arange(CHUNK)
    tri_fwd = (li[None, :] <= li[:, None]).astype(BF16)
    tri_bwd = (li[None, :] >= li[:, None]).astype(BF16)
    chan_head = jnp.arange(D_INNER) // SSM_HEADDIM
    e1 = (jnp.arange(N_SSM_HEADS)[:, None] == chan_head[None, :]).astype(BF16)
    e_rows = jnp.concatenate([e1] * (HEAD_LANE_COPIES - 1) + [jnp.zeros_like(e1)], axis=0)
    ext = jnp.arange(CHUNK + 2 * HALO)
    shift_prev = (ext[None, :] == li[:, None] + HALO - 1).astype(BF16)
    shift_next = (ext[None, :] == li[:, None] + HALO + 1).astype(BF16)
    a = jnp.tile(-jnp.exp(a_log[0].astype(F32)), (1, HEAD_LANE_COPIES))
    dtb = jnp.tile(dt_bias[0], (1, HEAD_LANE_COPIES))
    return dict(
        norm1_w=norm1_w[0][None, :], w_all=w_all,
        ones_blk=ones_blk,
        qw=jnp.tile(q_norm_w[0][perm], GQA_GROUP)[None, :],
        kw=jnp.tile(k_norm_w[0][perm], N_KV_HEADS)[None, :],
        conv_w=ssm_conv_w[0], conv_b=ssm_conv_b[0][None, :],
        dtb_f=dtb[0][None, :], dtb_b=dtb[1][None, :],
        a_f=a[0][None, :], a_b=a[1][None, :],
        tri_f=tri_fwd, tri_b=tri_bwd, e_rows=e_rows,
        shift_prev=shift_prev, shift_next=shift_next,
        dskip=jnp.repeat(d_skip[0], SSM_HEADDIM)[None, :], ssd_nw=ssd_norm_w[0][None, :],
        gate_b=gate_b[0][None, :], wa=w_attn_branch[0].astype(BF16),
        ws=w_ssd_branch[0].astype(BF16), wo=w_out[0].astype(BF16), norm2_w=norm2_w[0][None, :],
        w_up=w_up[0].astype(BF16), ffn_cw=ffn_conv_w[0], ffn_cb=ffn_conv_b[0][None, :],
        w_down=w_down[0].astype(BF16), w_ple=w_ple[0].astype(BF16),
        w_pg=w_ple_gate[0].astype(BF16), b_pg=b_ple_gate[0][None, :],
        final_w=final_norm_w[None, :])


def _trunk(x, p, prm):
    b, s, _ = x.shape
    t = b * s
    x2d = x.reshape(t, D_MODEL)
    q, k, v, z, xbc, gates, dt = _in_proj(x2d, prm["norm1_w"], prm["w_all"])
    cos, sin_signed = _rope_tables(s)

    ts = min(KV_PREP_TILE, s)
    k_heads, vt_heads = _kv_prep(k.reshape(b, s, KV_DIM), v.reshape(b, s, KV_DIM), cos, sin_signed,
                                 prm["ones_blk"], prm["kw"], ts)
    attn = _attention(q.reshape(b, s, ATTN_DIM), k_heads, vt_heads, cos, sin_signed,
                      prm["ones_blk"], prm["qw"], min(ATTN_Q_TILE, s))

    xc = _xbc_conv(xbc.reshape(b, s, CONV_DIM), prm["shift_prev"], prm["shift_next"],
                   prm["conv_w"], prm["conv_b"])
    dt3 = dt.reshape(b, s, 2 * V7X_LANES)
    y_fwd = _ssd_pass(xc, dt3, (prm["dtb_f"], prm["a_f"], prm["tri_f"], prm["e_rows"]),
                      reverse=False)
    ssd = _ssd_pass(xc, dt3, (prm["dtb_b"], prm["a_b"], prm["tri_b"], prm["e_rows"],
                              prm["dskip"], prm["ssd_nw"]),
                    reverse=True, extra=(z.reshape(b, s, D_INNER), y_fwd))

    x1, h2 = _merge(x2d, attn.reshape(t, ATTN_DIM), ssd.reshape(t, D_INNER), gates,
                    prm["gate_b"], prm["wa"], prm["ws"], prm["wo"], prm["norm2_w"])
    u = _ffn_up(h2, prm["w_up"])
    out = _ffn_down(u, x1, p[0].reshape(t, PLE_DIM), s, prm["shift_prev"], prm["shift_next"],
                    prm["ffn_cw"], prm["ffn_cb"], prm["w_down"], prm["w_ple"], prm["w_pg"],
                    prm["b_pg"], prm["final_w"])
    return out.reshape(b, s, D_MODEL)


def kernel(x_prompt, x_sample, p_prompt, p_sample, norm1_w, w_in, ssm_conv_w, ssm_conv_b, dt_bias,
           a_log, d_skip, ssd_norm_w, q_norm_w, k_norm_w, gate_b, w_attn_branch, w_ssd_branch,
           w_out, norm2_w, w_up, ffn_conv_w, ffn_conv_b, w_down, w_ple, w_ple_gate, b_ple_gate,
           final_norm_w):
    prm = _prepare(norm1_w, w_in, ssm_conv_w, ssm_conv_b, dt_bias, a_log, d_skip, ssd_norm_w,
                   q_norm_w, k_norm_w, gate_b, w_attn_branch, w_ssd_branch, w_out, norm2_w, w_up,
                   ffn_conv_w, ffn_conv_b, w_down, w_ple, w_ple_gate, b_ple_gate, final_norm_w)
    return (_trunk(x_prompt, p_prompt, prm), _trunk(x_sample, p_sample, prm))
```
